```python
import jax, jax.numpy as jnp
from jax import lax
import numpy as np

D_MODEL = 1024
BATCH = 8
SEQ = 4096
DEPTH = 2

D_CONV = D_MODEL // 2
D_POOL = D_MODEL // 2
CONV_K = 31
POOL_WINDOWS = (2, 4, 8, 16)
N_POOL_GROUPS = len(POOL_WINDOWS)
POOL_GROUP = D_POOL // N_POOL_GROUPS
POOL_OUT_GROUP = D_MODEL // N_POOL_GROUPS
N_BRANCHES = 2
D_IN = 2 * D_CONV + D_POOL + N_BRANCHES * D_MODEL

N_EXPERT_GROUPS = 4
EXPERTS_PER_GROUP = 8
N_EXPERTS = N_EXPERT_GROUPS * EXPERTS_PER_GROUP
TOP_K = 2
D_EXPERT = D_MODEL // 4

N_MOD = 6
EPS = 1e-6

kernel_name = "hybrid_conformer_pool_hmoe_adaln"


def rms_norm(x, g):
    xf = x.astype(jnp.float32)
    y = xf * lax.rsqrt(jnp.mean(xf * xf, axis=-1, keepdims=True) + EPS)
    return (y * g.astype(jnp.float32)).astype(x.dtype)


def layer_norm(x, g, b):
    xf = x.astype(jnp.float32)
    mu = jnp.mean(xf, axis=-1, keepdims=True)
    var = jnp.mean(jnp.square(xf - mu), axis=-1, keepdims=True)
    y = (xf - mu) * lax.rsqrt(var + EPS)
    return (y * g.astype(jnp.float32) + b.astype(jnp.float32)).astype(x.dtype)


def modulate(h, shift, scale):
    return h * (1 + scale[:, None, :]) + shift[:, None, :]


def conformer_conv_branch(u, conv_w, conv_b, ln_g, ln_b, w_pw, b_pw):
    a, g = jnp.split(u, 2, axis=-1)
    v = a * jax.nn.sigmoid(g)
    v = lax.conv_general_dilated(
        v, conv_w[:, None, :], window_strides=(1,), padding=[(CONV_K - 1, 0)],
        dimension_numbers=("NWC", "WIO", "NWC"), feature_group_count=D_CONV) + conv_b
    v = jax.nn.silu(layer_norm(v, ln_g, ln_b))
    return v @ w_pw + b_pw


def causal_window_mean(u, w):
    s_len = u.shape[1]
    cs = jnp.cumsum(u.astype(jnp.float32), axis=1)
    shifted = jnp.pad(cs, ((0, 0), (w, 0), (0, 0)))[:, :s_len]
    count = jnp.minimum(jnp.arange(1, s_len + 1), w).astype(jnp.float32)
    return ((cs - shifted) / count[None, :, None]).astype(u.dtype)


def pool_branch(u, pool_w, pool_scale):
    b, s, _ = u.shape
    ug = u.reshape(b, s, N_POOL_GROUPS, POOL_GROUP)
    pooled = jnp.stack([causal_window_mean(ug[:, :, gi], w)
                        for gi, w in enumerate(POOL_WINDOWS)], axis=2) - ug
    y = jnp.einsum("bsgc,gco->bsgo", pooled, pool_w).reshape(b, s, D_MODEL)
    return y * pool_scale


def hierarchical_moe(h, w_rg, b_rg, w_re, b_re, w_g, w_u, w_d):
    b, s, d = h.shape
    t = h.reshape(b * s, d)
    n_tok = t.shape[0]
    group_probs = jax.nn.softmax((t @ w_rg + b_rg).astype(jnp.float32), axis=-1)
    p_group, g_idx = lax.top_k(group_probs, 1)
    exp_logits = (t @ w_re + b_re).astype(jnp.float32).reshape(n_tok, N_EXPERT_GROUPS, EXPERTS_PER_GROUP)
    in_group = jnp.take_along_axis(exp_logits, g_idx[:, :, None], axis=1)[:, 0]
    p_exp, e_idx = lax.top_k(jax.nn.softmax(in_group, axis=-1), TOP_K)
    weights = p_group * p_exp / jnp.sum(p_exp, axis=-1, keepdims=True)
    expert_id = g_idx * EXPERTS_PER_GROUP + e_idx
    combine = jnp.sum(jax.nn.one_hot(expert_id, N_EXPERTS, dtype=jnp.float32) * weights[..., None],
                      axis=1).astype(h.dtype)
    out = jnp.zeros_like(t)
    for gi in range(N_EXPERT_GROUPS):
        sl = slice(gi * EXPERTS_PER_GROUP, (gi + 1) * EXPERTS_PER_GROUP)
        act = jax.nn.silu(jnp.einsum("td,edh->teh", t, w_g[sl])) * jnp.einsum("td,edh->teh", t, w_u[sl])
        out = out + jnp.einsum("teh,ehd->td", act * combine[:, sl, None], w_d[sl])
    return out.reshape(b, s, d)


def setup_inputs(seed: int = 0) -> dict:
    key = jax.random.key(seed)
    ks = jax.random.split(key, 24)
    f32 = jnp.float32
    nrm = lambda k, shape, scale: (jax.random.normal(k, shape, f32) * scale)
    return {
        "x": nrm(ks[0], (BATCH, SEQ, D_MODEL), 1.0),
        "c": nrm(ks[1], (BATCH, D_MODEL), 1.0),
        "mixer_norm_g": 1.0 + nrm(ks[2], (DEPTH, D_MODEL), 0.05),
        "w_ada": nrm(ks[3], (DEPTH, D_MODEL, N_MOD * D_MODEL), 0.5 * D_MODEL ** -0.5),
        "b_ada": nrm(ks[4], (DEPTH, N_MOD * D_MODEL), 0.02),
        "w_in": nrm(ks[5], (DEPTH, D_MODEL, D_IN), D_MODEL ** -0.5),
        "conv_w": nrm(ks[6], (DEPTH, CONV_K, D_CONV), CONV_K ** -0.5),
        "conv_b": nrm(ks[7], (DEPTH, D_CONV), 0.02),
        "conv_ln_g": 1.0 + nrm(ks[8], (DEPTH, D_CONV), 0.05),
        "conv_ln_b": nrm(ks[9], (DEPTH, D_CONV), 0.02),
        "w_conv_out": nrm(ks[10], (DEPTH, D_CONV, D_MODEL), D_CONV ** -0.5),
        "b_conv_out": nrm(ks[11], (DEPTH, D_MODEL), 0.02),
        "pool_w": nrm(ks[12], (DEPTH, N_POOL_GROUPS, POOL_GROUP, POOL_OUT_GROUP), POOL_GROUP ** -0.5),
        "pool_scale": 1.0 + nrm(ks[13], (DEPTH, D_MODEL), 0.1),
        "w_out": nrm(ks[14], (DEPTH, D_MODEL, D_MODEL), D_MODEL ** -0.5),
        "ffn_norm_g": 1.0 + nrm(ks[15], (DEPTH, D_MODEL), 0.05),
        "w_router_group": nrm(ks[16], (DEPTH, D_MODEL, N_EXPERT_GROUPS), D_MODEL ** -0.5),
        "b_router_group": nrm(ks[17], (DEPTH, N_EXPERT_GROUPS), 0.01),
        "w_router_expert": nrm(ks[18], (DEPTH, D_MODEL, N_EXPERTS), D_MODEL ** -0.5),
        "b_router_expert": nrm(ks[19], (DEPTH, N_EXPERTS), 0.01),
        "w_expert_gate": nrm(ks[20], (DEPTH, N_EXPERTS, D_MODEL, D_EXPERT), D_MODEL ** -0.5),
        "w_expert_up": nrm(ks[21], (DEPTH, N_EXPERTS, D_MODEL, D_EXPERT), D_MODEL ** -0.5),
        "w_expert_down": nrm(ks[22], (DEPTH, N_EXPERTS, D_EXPERT, D_MODEL), D_EXPERT ** -0.5),
        "final_norm_g": 1.0 + nrm(ks[23], (D_MODEL,), 0.05),
    }


def reference(x, c, mixer_norm_g, w_ada, b_ada, w_in, conv_w, conv_b, conv_ln_g, conv_ln_b,
              w_conv_out, b_conv_out, pool_w, pool_scale, w_out, ffn_norm_g,
              w_router_group, b_router_group, w_router_expert, b_router_expert,
              w_expert_gate, w_expert_up, w_expert_down, final_norm_g):
    c_act = jax.nn.silu(c)
    for l in range(DEPTH):
        mod = c_act @ w_ada[l] + b_ada[l]
        sh1, sc1, g1, sh2, sc2, g2 = jnp.split(mod, N_MOD, axis=-1)
        h = modulate(rms_norm(x, mixer_norm_g[l]), sh1, sc1)
        proj = h @ w_in[l]
        u_conv = proj[..., :2 * D_CONV]
        u_pool = proj[..., 2 * D_CONV:2 * D_CONV + D_POOL]
        gate_a, gate_b = jnp.split(jax.nn.sigmoid(proj[..., 2 * D_CONV + D_POOL:]), N_BRANCHES, axis=-1)
        y_a = conformer_conv_branch(u_conv, conv_w[l], conv_b[l], conv_ln_g[l], conv_ln_b[l],
                                    w_conv_out[l], b_conv_out[l])
        y_b = pool_branch(u_pool, pool_w[l], pool_scale[l])
        mixed = gate_a * y_a + gate_b * y_b
        x = x + g1[:, None, :] * (mixed @ w_out[l])
        h2 = modulate(rms_norm(x, ffn_norm_g[l]), sh2, sc2)
        x = x + g2[:, None, :] * hierarchical_moe(h2, w_router_group[l], b_router_group[l],
                                                  w_router_expert[l], b_router_expert[l],
                                                  w_expert_gate[l], w_expert_up[l], w_expert_down[l])
    return rms_norm(x, final_norm_g)
```

```python
import functools

import jax
import jax.numpy as jnp
from jax import lax
from jax.experimental import pallas as pl
from jax.experimental.pallas import tpu as pltpu

D_MODEL = 1024
D_CONV = 512
D_POOL = 512
CONV_K = 31
POOL_WINDOWS = (2, 4, 8, 16)
POOL_GROUP = 128
POOL_OUT_GROUP = 256
N_GROUPS = 4
PER_GROUP = 8
N_EXPERTS = 32
D_EXPERT = 256
N_MOD = 6
EPS = 1e-6

LANES = 128
SUBLANES = 8
HALO = 32
SEQ_TILE = 512
CONV_ROWS = 64
MOE_TILE = 1024
ROUTER_LANES = 128
GROUP_LANE0 = N_EXPERTS
VMEM_LIMIT = 56 * 1024 * 1024

_F32 = jnp.float32
_BF16 = jnp.bfloat16


def _sigmoid(v):
    return 1.0 / (1.0 + jnp.exp(-v))


def _silu(v):
    return v * _sigmoid(v)


def _rms_norm(v, g):
    return v * lax.rsqrt(jnp.mean(v * v, axis=-1, keepdims=True) + EPS) * g


def _ada_kernel(c_ref, w_ref, b_ref, o_ref):
    c_act = _silu(c_ref[...])
    o_ref[0] = jnp.dot(c_act, w_ref[0], preferred_element_type=_F32,
                       precision=lax.Precision.HIGHEST) + b_ref[0]


def _ada(c, w_ada, b_ada):
    depth, d, n = w_ada.shape
    tn = 1024
    return pl.pallas_call(
        _ada_kernel,
        grid=(depth, n // tn),
        in_specs=[
            pl.BlockSpec(c.shape, lambda l, j: (0, 0)),
            pl.BlockSpec((1, d, tn), lambda l, j: (l, 0, j)),
            pl.BlockSpec((1, 1, tn), lambda l, j: (l, 0, j)),
        ],
        out_specs=pl.BlockSpec((1, c.shape[0], tn), lambda l, j: (l, 0, j)),
        out_shape=jax.ShapeDtypeStruct((depth, c.shape[0], n), _F32),
        name="ada_mod",
    )(c, w_ada, b_ada.reshape(depth, 1, n))


def _mixer_kernel(x_ref, sh_ref, sc_ref, gt_ref, ng_ref, win_ref, cw_ref, cb_ref, lng_ref, lnb_ref,
                  wco_ref, bco_ref, pw_ref, ps_ref, wout_ref, o_ref,
                  e_ref, s_ref, conv_ref, pe_ref, p2_ref, p4_ref, p8_ref):
    ts = SEQ_TILE
    j = pl.program_id(1)

    @pl.when(j == 0)
    def _():
        e_ref[0:HALO, :] = jnp.zeros((HALO, D_CONV), _F32)
        pe_ref[0:HALO, :] = jnp.zeros((HALO, D_POOL), _F32)

    @pl.when(j > 0)
    def _():
        e_ref[0:HALO, :] = e_ref[ts:ts + HALO, :]
        pe_ref[0:HALO, :] = pe_ref[ts:ts + HALO, :]

    xt = x_ref[0]
    h = _rms_norm(xt, ng_ref[...]) * (1.0 + sc_ref[0]) + sh_ref[0]
    hb = h.astype(_BF16)

    a = jnp.dot(hb, win_ref[:, 0:D_CONV], preferred_element_type=_F32)
    g = jnp.dot(hb, win_ref[:, D_CONV:2 * D_CONV], preferred_element_type=_F32)
    e_ref[HALO:HALO + ts, :] = a * _sigmoid(g)
    for sft in range(1, SUBLANES):
        s_ref[sft - 1] = e_ref[sft:sft + ts + HALO - SUBLANES, :]

    def conv_rows(i, carry):
        r0 = pl.multiple_of(i * CONV_ROWS, CONV_ROWS)
        for c in range(D_CONV // LANES):
            cs = slice(c * LANES, (c + 1) * LANES)
            acc = jnp.broadcast_to(cb_ref[:, cs], (CONV_ROWS, LANES))
            for k in range(CONV_K):
                q, sft = divmod(HALO - (CONV_K - 1) + k, SUBLANES)
                if sft == 0:
                    tap = e_ref[pl.ds(r0 + SUBLANES * q, CONV_ROWS), cs]
                else:
                    tap = s_ref[sft - 1, pl.ds(r0 + SUBLANES * q, CONV_ROWS), cs]
                acc = acc + tap * cw_ref[k:k + 1, cs]
            conv_ref[pl.ds(r0, CONV_ROWS), cs] = acc
        return carry

    lax.fori_loop(0, ts // CONV_ROWS, conv_rows, 0)

    v = conv_ref[...]
    mu = jnp.mean(v, axis=-1, keepdims=True)
    vc = v - mu
    var = jnp.mean(vc * vc, axis=-1, keepdims=True)
    v = _silu(vc * lax.rsqrt(var + EPS) * lng_ref[...] + lnb_ref[...])
    y_a = jnp.dot(v.astype(_BF16), wco_ref[...], preferred_element_type=_F32) + bco_ref[...]

    u = jnp.dot(hb, win_ref[:, 2 * D_CONV:2 * D_CONV + D_POOL], preferred_element_type=_F32)
    pe_ref[HALO:HALO + ts, :] = u
    n_ext = ts + HALO
    p2_ref[8:n_ext, :] = pe_ref[8:n_ext, :] + pe_ref[7:n_ext - 1, :]
    p4_ref[16:n_ext, :] = p2_ref[16:n_ext, :] + p2_ref[14:n_ext - 2, :]
    p8_ref[24:n_ext, :] = p4_ref[24:n_ext, :] + p4_ref[20:n_ext - 4, :]
    pos = j * ts + lax.broadcasted_iota(jnp.int32, (ts, 1), 0) + 1
    y_b_parts = []
    for gi, w in enumerate(POOL_WINDOWS):
        cs = slice(gi * POOL_GROUP, (gi + 1) * POOL_GROUP)
        if w == 2:
            win_sum = p2_ref[HALO:n_ext, cs]
        elif w == 4:
            win_sum = p4_ref[HALO:n_ext, cs]
        elif w == 8:
            win_sum = p8_ref[HALO:n_ext, cs]
        else:
            win_sum = p8_ref[HALO:n_ext, cs] + p8_ref[HALO - 8:n_ext - 8, cs]
        count = jnp.minimum(pos, w).astype(_F32)
        pooled = win_sum / count - u[:, cs]
        y_b_parts.append(jnp.dot(pooled.astype(_BF16), pw_ref[gi], preferred_element_type=_F32))
    y_b = jnp.concatenate(y_b_parts, axis=-1) * ps_ref[...]

    gate_a = _sigmoid(jnp.dot(hb, win_ref[:, 1536:2560], preferred_element_type=_F32))
    mixed = gate_a * y_a
    gate_b = _sigmoid(jnp.dot(hb, win_ref[:, 2560:3584], preferred_element_type=_F32))
    mixed = mixed + gate_b * y_b
    out = jnp.dot(mixed.astype(_BF16), wout_ref[...], preferred_element_type=_F32)
    o_ref[0] = xt + gt_ref[0] * out


def _mixer(x, sh, sc, gt, norm_g, w_in, conv_w, conv_b, ln_g, ln_b, w_co, b_co, pool_w, pool_scale, w_out):
    b, s, d = x.shape
    ts = SEQ_TILE
    const2 = lambda bi, j: (0, 0)
    row = lambda a: a.reshape(1, -1)
    mod_spec = pl.BlockSpec((1, 1, d), lambda bi, j: (bi, 0, 0))
    return pl.pallas_call(
        _mixer_kernel,
        grid=(b, s // ts),
        in_specs=[
            pl.BlockSpec((1, ts, d), lambda bi, j: (bi, j, 0)),
            mod_spec, mod_spec, mod_spec,
            pl.BlockSpec((1, d), const2),
            pl.BlockSpec(w_in.shape, const2),
            pl.BlockSpec(conv_w.shape, const2),
            pl.BlockSpec((1, D_CONV), const2),
            pl.BlockSpec((1, D_CONV), const2),
            pl.BlockSpec((1, D_CONV), const2),
            pl.BlockSpec(w_co.shape, const2),
            pl.BlockSpec((1, d), const2),
            pl.BlockSpec(pool_w.shape, lambda bi, j: (0, 0, 0)),
            pl.BlockSpec((1, d), const2),
            pl.BlockSpec(w_out.shape, const2),
        ],
        out_specs=pl.BlockSpec((1, ts, d), lambda bi, j: (bi, j, 0)),
        out_shape=jax.ShapeDtypeStruct(x.shape, _F32),
        scratch_shapes=[
            pltpu.VMEM((ts + HALO, D_CONV), _F32),
            pltpu.VMEM((SUBLANES - 1, ts + HALO - SUBLANES, D_CONV), _F32),
            pltpu.VMEM((ts, D_CONV), _F32),
            pltpu.VMEM((ts + HALO, D_POOL), _F32),
            pltpu.VMEM((ts + HALO, D_POOL), _F32),
            pltpu.VMEM((ts + HALO, D_POOL), _F32),
            pltpu.VMEM((ts + HALO, D_POOL), _F32),
        ],
        compiler_params=pltpu.CompilerParams(
            dimension_semantics=("arbitrary", "arbitrary"), vmem_limit_bytes=VMEM_LIMIT),
        name="mixer",
    )(x, sh, sc, gt, row(norm_g), w_in, conv_w, row(conv_b), row(ln_g), row(ln_b), w_co, row(b_co),
      pool_w, row(pool_scale), w_out)


def _moe_kernel(x_ref, sh_ref, sc_ref, gt_ref, ng_ref, wr_ref, br_ref, wgu_ref, wd_ref, fg_ref, o_ref,
                hb_ref, comb_ref, acc_ref, *, final_norm):
    e = pl.program_id(1)
    lane = lax.broadcasted_iota(jnp.int32, (MOE_TILE, ROUTER_LANES), 1)

    @pl.when(e == 0)
    def _():
        h = _rms_norm(x_ref[...], ng_ref[...]) * (1.0 + sc_ref[0]) + sh_ref[0]
        hb_ref[...] = h.astype(_BF16)
        logits = jnp.dot(h, wr_ref[...], preferred_element_type=_F32,
                         precision=lax.Precision.HIGHEST) + br_ref[...]
        neg = jnp.float32(-jnp.inf)
        big = jnp.int32(ROUTER_LANES)
        is_group = (lane >= GROUP_LANE0) & (lane < GROUP_LANE0 + N_GROUPS)
        glog = jnp.where(is_group, logits, neg)
        gmax = jnp.max(glog, axis=-1, keepdims=True)
        p_group = 1.0 / jnp.sum(jnp.exp(glog - gmax), axis=-1, keepdims=True)
        g_idx = jnp.min(jnp.where(glog == gmax, lane - GROUP_LANE0, big), axis=-1, keepdims=True)
        in_group = (lane < N_EXPERTS) & ((lane // PER_GROUP) == g_idx)
        el = jnp.where(in_group, logits, neg)
        m1 = jnp.max(el, axis=-1, keepdims=True)
        i1 = jnp.min(jnp.where(el == m1, lane, big), axis=-1, keepdims=True)
        el2 = jnp.where(lane == i1, neg, el)
        m2 = jnp.max(el2, axis=-1, keepdims=True)
        i2 = jnp.min(jnp.where(el2 == m2, lane, big), axis=-1, keepdims=True)
        r = jnp.exp(m2 - m1)
        w1 = p_group / (1.0 + r)
        w2 = w1 * r
        comb_ref[...] = jnp.where(lane == i1, w1, 0.0) + jnp.where(lane == i2, w2, 0.0)
        acc_ref[...] = jnp.zeros_like(acc_ref)

    ce = jnp.sum(jnp.where(lane == e, comb_ref[...], 0.0), axis=-1, keepdims=True)
    hgu = jnp.dot(hb_ref[...], wgu_ref[0], preferred_element_type=_F32)
    act = _silu(hgu[:, :D_EXPERT]) * hgu[:, D_EXPERT:] * ce
    acc_ref[...] += jnp.dot(act.astype(_BF16), wd_ref[0], preferred_element_type=_F32)

    @pl.when(e == N_EXPERTS - 1)
    def _():
        y = x_ref[...] + gt_ref[0] * acc_ref[...]
        if final_norm:
            y = _rms_norm(y, fg_ref[...])
        o_ref[...] = y


def _moe(x, sh, sc, gt, norm_g, w_r, b_r, w_gu, w_d, final_g, final_norm):
    b, s, d = x.shape
    t = b * s
    tm = MOE_TILE
    per_b = s // tm
    xf = x.reshape(t, d)
    const2 = lambda i, e: (0, 0)
    mod_spec = pl.BlockSpec((1, 1, d), lambda i, e: (i // per_b, 0, 0))
    out = pl.pallas_call(
        functools.partial(_moe_kernel, final_norm=final_norm),
        grid=(t // tm, N_EXPERTS),
        in_specs=[
            pl.BlockSpec((tm, d), lambda i, e: (i, 0)),
            mod_spec, mod_spec, mod_spec,
            pl.BlockSpec((1, d), const2),
            pl.BlockSpec(w_r.shape, const2),
            pl.BlockSpec((1, ROUTER_LANES), const2),
            pl.BlockSpec((1, d, 2 * D_EXPERT), lambda i, e: (e, 0, 0)),
            pl.BlockSpec((1, D_EXPERT, d), lambda i, e: (e, 0, 0)),
            pl.BlockSpec((1, d), const2),
        ],
        out_specs=pl.BlockSpec((tm, d), lambda i, e: (i, 0)),
        out_shape=jax.ShapeDtypeStruct((t, d), _F32),
        scratch_shapes=[
            pltpu.VMEM((tm, d), _BF16),
            pltpu.VMEM((tm, ROUTER_LANES), _F32),
            pltpu.VMEM((tm, d), _F32),
        ],
        compiler_params=pltpu.CompilerParams(
            dimension_semantics=("arbitrary", "arbitrary"), vmem_limit_bytes=VMEM_LIMIT),
        name="moe",
    )(xf, sh, sc, gt, norm_g.reshape(1, d), w_r, b_r, w_gu, w_d, final_g.reshape(1, d))
    return out.reshape(b, s, d)


def _router_params(w_rg, b_rg, w_re, b_re):
    d = w_rg.shape[0]
    pad = ROUTER_LANES - N_EXPERTS - N_GROUPS
    w = jnp.concatenate([w_re, w_rg, jnp.zeros((d, pad), _F32)], axis=1)
    b = jnp.concatenate([b_re, b_rg, jnp.zeros((pad,), _F32)]).reshape(1, ROUTER_LANES)
    return w, b


def kernel(x, c, mixer_norm_g, w_ada, b_ada, w_in, conv_w, conv_b, conv_ln_g, conv_ln_b, w_conv_out,
           b_conv_out, pool_w, pool_scale, w_out, ffn_norm_g, w_router_group, b_router_group,
           w_router_expert, b_router_expert, w_expert_gate, w_expert_up, w_expert_down, final_norm_g):
    depth = w_ada.shape[0]
    bsz = x.shape[0]
    mod = _ada(c, w_ada, b_ada)
    mod = mod.reshape(depth, bsz, N_MOD, 1, D_MODEL)
    for l in range(depth):
        sh1, sc1, g1, sh2, sc2, g2 = (mod[l, :, i] for i in range(N_MOD))
        x = _mixer(x, sh1, sc1, g1, mixer_norm_g[l], w_in[l].astype(_BF16), conv_w[l], conv_b[l],
                   conv_ln_g[l], conv_ln_b[l], w_conv_out[l].astype(_BF16), b_conv_out[l],
                   pool_w[l].astype(_BF16), pool_scale[l], w_out[l].astype(_BF16))
        w_r, b_r = _router_params(w_router_group[l], b_router_group[l],
                                  w_router_expert[l], b_router_expert[l])
        w_gu = jnp.concatenate([w_expert_gate[l], w_expert_up[l]], axis=-1).astype(_BF16)
        x = _moe(x, sh2, sc2, g2, ffn_norm_g[l], w_r, b_r, w_gu, w_expert_down[l].astype(_BF16),
                 final_norm_g, final_norm=(l == depth - 1))
    return x
```

```python
import functools

import jax
import jax.numpy as jnp
from jax import lax
from jax.experimental import pallas as pl
from jax.experimental.pallas import tpu as pltpu

D_MODEL = 1024
D_CONV = 512
D_POOL = 512
CONV_K = 31
POOL_WINDOWS = (2, 4, 8, 16)
POOL_GROUP = 128
POOL_OUT_GROUP = 256
N_GROUPS = 4
PER_GROUP = 8
N_EXPERTS = 32
TOP_K = 2
D_EXPERT = 256
N_MOD = 6
EPS = 1e-6

COL_POOL = 2 * D_CONV
COL_GATE_A = COL_POOL + D_POOL
COL_GATE_B = COL_GATE_A + D_MODEL
D_IN = COL_GATE_B + D_MODEL

LANES = 128
SUBLANES = 8
ROW_VREGS = D_MODEL // LANES
HALO = 32
SEQ_TILE = 512
CONV_ROWS = 64
MOE_TILE = 1024
MOE_CHUNK = 128
CUM_BLOCK = 256
ROUTER_ROWS = 40
GROUP_ROW0 = N_EXPERTS
UNROLL = 8
VMEM_LIMIT = 56 * 1024 * 1024

_F32 = jnp.float32
_BF16 = jnp.bfloat16
assert ROW_VREGS == SUBLANES


def _sigmoid(v):
    return 1.0 / (1.0 + jnp.exp(-v))


def _silu(v):
    return v * _sigmoid(v)


def _rms_norm(v, g):
    return v * lax.rsqrt(jnp.mean(v * v, axis=-1, keepdims=True) + EPS) * g


def _ada_kernel(c_ref, w_ref, b_ref, o_ref):
    c_act = _silu(c_ref[...])
    o_ref[0] = jnp.dot(c_act, w_ref[0], preferred_element_type=_F32,
                       precision=lax.Precision.HIGHEST) + b_ref[0]


def _ada(c, w_ada, b_ada):
    depth, d, n = w_ada.shape
    tn = 1024
    return pl.pallas_call(
        _ada_kernel,
        grid=(depth, n // tn),
        in_specs=[
            pl.BlockSpec(c.shape, lambda l, j: (0, 0)),
            pl.BlockSpec((1, d, tn), lambda l, j: (l, 0, j)),
            pl.BlockSpec((1, 1, tn), lambda l, j: (l, 0, j)),
        ],
        out_specs=pl.BlockSpec((1, c.shape[0], tn), lambda l, j: (l, 0, j)),
        out_shape=jax.ShapeDtypeStruct((depth, c.shape[0], n), _F32),
        name="ada_mod",
    )(c, w_ada, b_ada.reshape(depth, 1, n))


def _mixer_kernel(x_ref, sh_ref, sc_ref, gt_ref, ng_ref, win_ref, cw_ref, cb_ref, lng_ref, lnb_ref,
                  wco_ref, bco_ref, pw_ref, ps_ref, wout_ref, o_ref,
                  e_ref, s_ref, conv_ref, pe_ref, p2_ref, p4_ref, p8_ref):
    ts = SEQ_TILE
    j = pl.program_id(1)

    @pl.when(j == 0)
    def _():
        e_ref[0:HALO, :] = jnp.zeros((HALO, D_CONV), _F32)
        pe_ref[0:HALO, :] = jnp.zeros((HALO, D_POOL), _F32)

    @pl.when(j > 0)
    def _():
        e_ref[0:HALO, :] = e_ref[ts:ts + HALO, :]
        pe_ref[0:HALO, :] = pe_ref[ts:ts + HALO, :]

    xt = x_ref[0]
    h = _rms_norm(xt, ng_ref[...]) * (1.0 + sc_ref[0]) + sh_ref[0]
    hb = h.astype(_BF16)

    a = jnp.dot(hb, win_ref[:, 0:D_CONV], preferred_element_type=_F32)
    g = jnp.dot(hb, win_ref[:, D_CONV:2 * D_CONV], preferred_element_type=_F32)
    e_ref[HALO:HALO + ts, :] = a * _sigmoid(g)
    for sft in range(1, SUBLANES):
        s_ref[sft - 1] = e_ref[sft:sft + ts + HALO - SUBLANES, :]

    def conv_rows(i, carry):
        r0 = pl.multiple_of(i * CONV_ROWS, CONV_ROWS)
        for c in range(D_CONV // LANES):
            cs = slice(c * LANES, (c + 1) * LANES)
            acc = jnp.broadcast_to(cb_ref[:, cs], (CONV_ROWS, LANES))
            for k in range(CONV_K):
                q, sft = divmod(HALO - (CONV_K - 1) + k, SUBLANES)
                if sft == 0:
                    tap = e_ref[pl.ds(r0 + SUBLANES * q, CONV_ROWS), cs]
                else:
                    tap = s_ref[sft - 1, pl.ds(r0 + SUBLANES * q, CONV_ROWS), cs]
                acc = acc + tap * cw_ref[k:k + 1, cs]
            conv_ref[pl.ds(r0, CONV_ROWS), cs] = acc
        return carry

    lax.fori_loop(0, ts // CONV_ROWS, conv_rows, 0)

    v = conv_ref[...]
    mu = jnp.mean(v, axis=-1, keepdims=True)
    vc = v - mu
    var = jnp.mean(vc * vc, axis=-1, keepdims=True)
    v = _silu(vc * lax.rsqrt(var + EPS) * lng_ref[...] + lnb_ref[...])
    y_a = jnp.dot(v.astype(_BF16), wco_ref[...], preferred_element_type=_F32) + bco_ref[...]

    u = jnp.dot(hb, win_ref[:, COL_POOL:COL_GATE_A], preferred_element_type=_F32)
    pe_ref[HALO:HALO + ts, :] = u
    n_ext = ts + HALO
    p2_ref[8:n_ext, :] = pe_ref[8:n_ext, :] + pe_ref[7:n_ext - 1, :]
    p4_ref[16:n_ext, :] = p2_ref[16:n_ext, :] + p2_ref[14:n_ext - 2, :]
    p8_ref[24:n_ext, :] = p4_ref[24:n_ext, :] + p4_ref[20:n_ext - 4, :]
    pos = j * ts + lax.broadcasted_iota(jnp.int32, (ts, 1), 0) + 1
    y_b_parts = []
    for gi, w in enumerate(POOL_WINDOWS):
        cs = slice(gi * POOL_GROUP, (gi + 1) * POOL_GROUP)
        if w == 2:
            win_sum = p2_ref[HALO:n_ext, cs]
        elif w == 4:
            win_sum = p4_ref[HALO:n_ext, cs]
        elif w == 8:
            win_sum = p8_ref[HALO:n_ext, cs]
        else:
            win_sum = p8_ref[HALO:n_ext, cs] + p8_ref[HALO - 8:n_ext - 8, cs]
        count = jnp.minimum(pos, w).astype(_F32)
        pooled = win_sum / count - u[:, cs]
        y_b_parts.append(jnp.dot(pooled.astype(_BF16), pw_ref[gi], preferred_element_type=_F32))
    y_b = jnp.concatenate(y_b_parts, axis=-1) * ps_ref[...]

    gate_a = _sigmoid(jnp.dot(hb, win_ref[:, COL_GATE_A:COL_GATE_B], preferred_element_type=_F32))
    mixed = gate_a * y_a
    gate_b = _sigmoid(jnp.dot(hb, win_ref[:, COL_GATE_B:D_IN], preferred_element_type=_F32))
    mixed = mixed + gate_b * y_b
    out = jnp.dot(mixed.astype(_BF16), wout_ref[...], preferred_element_type=_F32)
    o_ref[0] = xt + gt_ref[0] * out


def _mixer(x, sh, sc, gt, norm_g, w_in, conv_w, conv_b, ln_g, ln_b, w_co, b_co, pool_w, pool_scale, w_out):
    b, s, d = x.shape
    ts = SEQ_TILE
    const2 = lambda bi, j: (0, 0)
    row = lambda a: a.reshape(1, -1)
    mod_spec = pl.BlockSpec((1, 1, d), lambda bi, j: (bi, 0, 0))
    return pl.pallas_call(
        _mixer_kernel,
        grid=(b, s // ts),
        in_specs=[
            pl.BlockSpec((1, ts, d), lambda bi, j: (bi, j, 0)),
            mod_spec, mod_spec, mod_spec,
            pl.BlockSpec((1, d), const2),
            pl.BlockSpec(w_in.shape, const2),
            pl.BlockSpec(conv_w.shape, const2),
            pl.BlockSpec((1, D_CONV), const2),
            pl.BlockSpec((1, D_CONV), const2),
            pl.BlockSpec((1, D_CONV), const2),
            pl.BlockSpec(w_co.shape, const2),
            pl.BlockSpec((1, d), const2),
            pl.BlockSpec(pool_w.shape, lambda bi, j: (0, 0, 0)),
            pl.BlockSpec((1, d), const2),
            pl.BlockSpec(w_out.shape, const2),
        ],
        out_specs=pl.BlockSpec((1, ts, d), lambda bi, j: (bi, j, 0)),
        out_shape=jax.ShapeDtypeStruct(x.shape, _F32),
        scratch_shapes=[
            pltpu.VMEM((ts + HALO, D_CONV), _F32),
            pltpu.VMEM((SUBLANES - 1, ts + HALO - SUBLANES, D_CONV), _F32),
            pltpu.VMEM((ts, D_CONV), _F32),
            pltpu.VMEM((ts + HALO, D_POOL), _F32),
            pltpu.VMEM((ts + HALO, D_POOL), _F32),
            pltpu.VMEM((ts + HALO, D_POOL), _F32),
            pltpu.VMEM((ts + HALO, D_POOL), _F32),
        ],
        compiler_params=pltpu.CompilerParams(
            dimension_semantics=("arbitrary", "arbitrary"), vmem_limit_bytes=VMEM_LIMIT),
        name="mixer",
    )(x, sh, sc, gt, row(norm_g), w_in, conv_w, row(conv_b), row(ln_g), row(ln_b), w_co, row(b_co),
      pool_w, row(pool_scale), w_out)


def _route_kernel(x_ref, sh_ref, sc_ref, ng_ref, wrt_ref, brt_ref, pos_ref, wgt_ref, seg_ref):
    tm = MOE_TILE
    h = _rms_norm(x_ref[...], ng_ref[...]) * (1.0 + sc_ref[0]) + sh_ref[0]
    logits = lax.dot_general(wrt_ref[...], h, (((1,), (1,)), ((), ())), preferred_element_type=_F32,
                             precision=lax.Precision.HIGHEST) + brt_ref[...]
    neg = jnp.float32(-jnp.inf)
    big = jnp.int32(N_EXPERTS)

    grow = lax.broadcasted_iota(jnp.int32, (SUBLANES, tm), 0)
    glog = jnp.where(grow < N_GROUPS, logits[GROUP_ROW0:GROUP_ROW0 + SUBLANES, :], neg)
    gmax = jnp.max(glog, axis=0, keepdims=True)
    p_group = 1.0 / jnp.sum(jnp.exp(glog - gmax), axis=0, keepdims=True)
    g_idx = jnp.min(jnp.where(glog == gmax, grow, big), axis=0, keepdims=True)

    erow = lax.broadcasted_iota(jnp.int32, (N_EXPERTS, tm), 0)
    el = jnp.where((erow // PER_GROUP) == g_idx, logits[0:N_EXPERTS, :], neg)
    m1 = jnp.max(el, axis=0, keepdims=True)
    i1 = jnp.min(jnp.where(el == m1, erow, big), axis=0, keepdims=True)
    el2 = jnp.where(erow == i1, neg, el)
    m2 = jnp.max(el2, axis=0, keepdims=True)
    i2 = jnp.min(jnp.where(el2 == m2, erow, big), axis=0, keepdims=True)
    r = jnp.exp(m2 - m1)
    w1 = p_group / (1.0 + r)
    w2 = w1 * r
    sel1 = erow == i1
    sel2 = erow == i2

    used = jnp.where(sel1 | sel2, 1.0, 0.0)
    tri_r = lax.broadcasted_iota(jnp.int32, (CUM_BLOCK, CUM_BLOCK), 0)
    tri_c = lax.broadcasted_iota(jnp.int32, (CUM_BLOCK, CUM_BLOCK), 1)
    upper = jnp.where(tri_r < tri_c, 1.0, 0.0).astype(_BF16)
    run = jnp.zeros((N_EXPERTS, 1), _F32)
    ranks = []
    for blk in range(tm // CUM_BLOCK):
        ub = used[:, blk * CUM_BLOCK:(blk + 1) * CUM_BLOCK]
        ranks.append(jnp.dot(ub.astype(_BF16), upper, preferred_element_type=_F32) + run)
        run = run + jnp.sum(ub, axis=1, keepdims=True)
    rank = jnp.concatenate(ranks, axis=1)
    cnt = jnp.concatenate([jnp.broadcast_to(run, (N_EXPERTS, LANES)),
                           jnp.zeros((LANES - N_EXPERTS, LANES), _F32)], axis=0)
    low_r = lax.broadcasted_iota(jnp.int32, (LANES, LANES), 0)
    low_c = lax.broadcasted_iota(jnp.int32, (LANES, LANES), 1)
    lower = jnp.where(low_c < low_r, 1.0, 0.0)
    off = jnp.dot(lower, cnt, preferred_element_type=_F32, precision=lax.Precision.HIGHEST)
    slot = off[0:N_EXPERTS, 0:1] + rank
    pos_ref[0, 0:1, :] = jnp.sum(jnp.where(sel1, slot, 0.0), axis=0, keepdims=True).astype(jnp.int32)
    pos_ref[0, 1:2, :] = jnp.sum(jnp.where(sel2, slot, 0.0), axis=0, keepdims=True).astype(jnp.int32)
    wgt_ref[0, 0:1, :] = w1
    wgt_ref[0, 1:2, :] = w2
    seg_ref[0, 0:N_EXPERTS, :] = off[0:N_EXPERTS, :].astype(jnp.int32)
    seg_ref[0, N_EXPERTS:2 * N_EXPERTS, :] = cnt[0:N_EXPERTS, :].astype(jnp.int32)


def _route(xf, sh, sc, norm_g, w_rt, b_rt, per_b):
    t, d = xf.shape
    tm = MOE_TILE
    n_tiles = t // tm
    const2 = lambda i: (0, 0)
    mod_spec = pl.BlockSpec((1, 1, d), lambda i: (i // per_b, 0, 0))
    out3 = lambda i: (i, 0, 0)
    return pl.pallas_call(
        _route_kernel,
        grid=(n_tiles,),
        in_specs=[
            pl.BlockSpec((tm, d), lambda i: (i, 0)),
            mod_spec, mod_spec,
            pl.BlockSpec((1, d), const2),
            pl.BlockSpec((ROUTER_ROWS, d), const2),
            pl.BlockSpec((ROUTER_ROWS, 1), const2),
        ],
        out_specs=[
            pl.BlockSpec((1, TOP_K, tm), out3),
            pl.BlockSpec((1, TOP_K, tm), out3),
            pl.BlockSpec((1, 2 * N_EXPERTS, LANES), out3),
        ],
        out_shape=[
            jax.ShapeDtypeStruct((n_tiles, TOP_K, tm), jnp.int32),
            jax.ShapeDtypeStruct((n_tiles, TOP_K, tm), _F32),
            jax.ShapeDtypeStruct((n_tiles, 2 * N_EXPERTS, LANES), jnp.int32),
        ],
        compiler_params=pltpu.CompilerParams(
            dimension_semantics=("arbitrary",), vmem_limit_bytes=VMEM_LIMIT),
        name="route",
    )(xf, sh, sc, norm_g.reshape(1, d), w_rt, b_rt)


def _row(ref, r):
    return ref[pl.ds(pl.multiple_of(r * ROW_VREGS, ROW_VREGS), ROW_VREGS), :]


def _moe_kernel(pos_s, wgt_s, seg_s,
                x_ref, sh_ref, sc_ref, gt_ref, ng_ref, fg_ref, wgu_hbm, wd_hbm, o_ref,
                tm_ref, srt_ref, wgu_buf, wd_buf, sem, *, final_norm):
    tm = MOE_TILE
    ch = MOE_CHUNK
    i = pl.program_id(0)
    base = i * (TOP_K * tm)

    def weight_copies(e, slot):
        return (pltpu.make_async_copy(wgu_hbm.at[e], wgu_buf.at[slot], sem.at[0, slot]),
                pltpu.make_async_copy(wd_hbm.at[e], wd_buf.at[slot], sem.at[1, slot]))

    for cp in weight_copies(0, 0):
        cp.start()

    h = _rms_norm(x_ref[...], ng_ref[...]) * (1.0 + sc_ref[0]) + sh_ref[0]
    for s in range(ROW_VREGS):
        tm_ref[pl.ds(s, tm, stride=ROW_VREGS), :] = h[:, s * LANES:(s + 1) * LANES]
    srt_ref[TOP_K * tm * ROW_VREGS:, :] = jnp.zeros((ch * ROW_VREGS, LANES), _F32)

    def dispatch(it, carry):
        t0 = it * UNROLL
        rows = [_row(tm_ref, t0 + u) for u in range(UNROLL)]
        for u in range(UNROLL):
            for k in range(TOP_K):
                p = pos_s[base + k * tm + t0 + u]
                srt_ref[pl.ds(pl.multiple_of(p * ROW_VREGS, ROW_VREGS), ROW_VREGS), :] = rows[u]
        return carry

    lax.fori_loop(0, tm // UNROLL, dispatch, 0)

    def expert(e, carry):
        slot = e % 2
        for cp in weight_copies(e, slot):
            cp.wait()

        @pl.when(e + 1 < N_EXPERTS)
        def _():
            for cp in weight_copies(e + 1, 1 - slot):
                cp.start()

        off = seg_s[i * (2 * N_EXPERTS) + e]
        cnt = seg_s[i * (2 * N_EXPERTS) + N_EXPERTS + e]

        def chunk(c, inner):
            r0 = (off + c * ch) * ROW_VREGS
            xs = jnp.concatenate(
                [srt_ref[pl.ds(r0 + s, ch, stride=ROW_VREGS), :] for s in range(ROW_VREGS)], axis=1)
            hgu = jnp.dot(xs.astype(_BF16), wgu_buf[slot], preferred_element_type=_F32)
            act = _silu(hgu[:, :D_EXPERT]) * hgu[:, D_EXPERT:]
            y = jnp.dot(act.astype(_BF16), wd_buf[slot], preferred_element_type=_F32)
            valid = lax.broadcasted_iota(jnp.int32, (ch, 1), 0) < (cnt - c * ch)
            for s in range(ROW_VREGS):
                cs = slice(s * LANES, (s + 1) * LANES)
                srt_ref[pl.ds(r0 + s, ch, stride=ROW_VREGS), :] = jnp.where(valid, y[:, cs], xs[:, cs])
            return inner

        lax.fori_loop(0, (cnt + ch - 1) // ch, chunk, 0)
        return carry

    lax.fori_loop(0, N_EXPERTS, expert, 0)

    def combine(it, carry):
        t0 = it * UNROLL
        outs = []
        for u in range(UNROLL):
            t = t0 + u
            y0 = _row(srt_ref, pos_s[base + t])
            y1 = _row(srt_ref, pos_s[base + tm + t])
            outs.append(wgt_s[base + t] * y0 + wgt_s[base + tm + t] * y1)
        for u in range(UNROLL):
            tm_ref[pl.ds(pl.multiple_of((t0 + u) * ROW_VREGS, ROW_VREGS), ROW_VREGS), :] = outs[u]
        return carry

    lax.fori_loop(0, tm // UNROLL, combine, 0)

    moe = jnp.concatenate(
        [tm_ref[pl.ds(s, tm, stride=ROW_VREGS), :] for s in range(ROW_VREGS)], axis=1)
    y = x_ref[...] + gt_ref[0] * moe
    if final_norm:
        y = _rms_norm(y, fg_ref[...])
    o_ref[...] = y


def _moe(x, sh, sc, gt, norm_g, w_rt, b_rt, w_gu, w_d, final_g, final_norm):
    b, s, d = x.shape
    t = b * s
    tm = MOE_TILE
    per_b = s // tm
    n_tiles = t // tm
    xf = x.reshape(t, d)
    pos, wgt, seg = _route(xf, sh, sc, norm_g, w_rt, b_rt, per_b)
    const2 = lambda i, *_: (0, 0)
    mod_spec = pl.BlockSpec((1, 1, d), lambda i, *_: (i // per_b, 0, 0))
    out = pl.pallas_call(
        functools.partial(_moe_kernel, final_norm=final_norm),
        grid_spec=pltpu.PrefetchScalarGridSpec(
            num_scalar_prefetch=3,
            grid=(n_tiles,),
            in_specs=[
                pl.BlockSpec((tm, d), lambda i, *_: (i, 0)),
                mod_spec, mod_spec, mod_spec,
                pl.BlockSpec((1, d), const2),
                pl.BlockSpec((1, d), const2),
                pl.BlockSpec(memory_space=pl.ANY),
                pl.BlockSpec(memory_space=pl.ANY),
            ],
            out_specs=pl.BlockSpec((tm, d), lambda i, *_: (i, 0)),
            scratch_shapes=[
                pltpu.VMEM((tm * ROW_VREGS, LANES), _F32),
                pltpu.VMEM(((TOP_K * tm + MOE_CHUNK) * ROW_VREGS, LANES), _F32),
                pltpu.VMEM((2, d, 2 * D_EXPERT), _BF16),
                pltpu.VMEM((2, D_EXPERT, d), _BF16),
                pltpu.SemaphoreType.DMA((2, 2)),
            ],
        ),
        out_shape=jax.ShapeDtypeStruct((t, d), _F32),
        compiler_params=pltpu.CompilerParams(
            dimension_semantics=("arbitrary",), vmem_limit_bytes=VMEM_LIMIT),
        name="moe",
    )(pos.reshape(-1), wgt.reshape(-1), seg[:, :, 0].reshape(-1),
      xf, sh, sc, gt, norm_g.reshape(1, d), final_g.reshape(1, d), w_gu, w_d)
    return out.reshape(b, s, d)


def _router_params(w_rg, b_rg, w_re, b_re):
    d = w_rg.shape[0]
    pad = ROUTER_ROWS - N_EXPERTS - N_GROUPS
    w = jnp.concatenate([w_re.T, w_rg.T, jnp.zeros((pad, d), _F32)], axis=0)
    b = jnp.concatenate([b_re, b_rg, jnp.zeros((pad,), _F32)]).reshape(ROUTER_ROWS, 1)
    return w, b


def kernel(x, c, mixer_norm_g, w_ada, b_ada, w_in, conv_w, conv_b, conv_ln_g, conv_ln_b, w_conv_out,
           b_conv_out, pool_w, pool_scale, w_out, ffn_norm_g, w_router_group, b_router_group,
           w_router_expert, b_router_expert, w_expert_gate, w_expert_up, w_expert_down, final_norm_g):
    depth = w_ada.shape[0]
    bsz = x.shape[0]
    mod = _ada(c, w_ada, b_ada)
    mod = mod.reshape(depth, bsz, N_MOD, 1, D_MODEL)
    for l in range(depth):
        sh1, sc1, g1, sh2, sc2, g2 = (mod[l, :, i] for i in range(N_MOD))
        x = _mixer(x, sh1, sc1, g1, mixer_norm_g[l], w_in[l].astype(_BF16), conv_w[l], conv_b[l],
                   conv_ln_g[l], conv_ln_b[l], w_conv_out[l].astype(_BF16), b_conv_out[l],
                   pool_w[l].astype(_BF16), pool_scale[l], w_out[l].astype(_BF16))
        w_rt, b_rt = _router_params(w_router_group[l], b_router_group[l],
                                    w_router_expert[l], b_router_expert[l])
        w_gu = jnp.concatenate([w_expert_gate[l], w_expert_up[l]], axis=-1).astype(_BF16)
        x = _moe(x, sh2, sc2, g2, ffn_norm_g[l], w_rt, b_rt, w_gu, w_expert_down[l].astype(_BF16),
                 final_norm_g, final_norm=(l == depth - 1))
    return x
```

```python
import functools

import jax
import jax.numpy as jnp
from jax import lax
from jax.experimental import pallas as pl
from jax.experimental.pallas import tpu as pltpu

D_MODEL = 1024
D_CONV = 512
D_POOL = 512
CONV_K = 31
POOL_WINDOWS = (2, 4, 8, 16)
POOL_GROUP = 128
POOL_OUT_GROUP = 256
N_GROUPS = 4
PER_GROUP = 8
N_EXPERTS = 32
TOP_K = 2
D_EXPERT = 256
N_MOD = 6
EPS = 1e-6

COL_POOL = 2 * D_CONV
COL_GATE_A = COL_POOL + D_POOL
COL_GATE_B = COL_GATE_A + D_MODEL
D_IN = COL_GATE_B + D_MODEL

LANES = 128
SUBLANES = 8
ROW_VREGS = D_MODEL // LANES
HALO = 32
SEQ_TILE = 512
CONV_ROWS = 64
MOE_TILE = 4096
MOE_BLOCK = 512
MOE_BLOCKS = MOE_TILE // MOE_BLOCK
MOE_CHUNK = 256
CUM_BLOCK = 256
ROUTER_ROWS = 40
GROUP_ROW0 = N_EXPERTS
UNROLL = 8
VMEM_LIMIT = 56 * 1024 * 1024

_F32 = jnp.float32
_BF16 = jnp.bfloat16
assert ROW_VREGS == SUBLANES


def _sigmoid(v):
    return 1.0 / (1.0 + jnp.exp(-v))


def _silu(v):
    return v * _sigmoid(v)


def _rms_norm(v, g):
    return v * lax.rsqrt(jnp.mean(v * v, axis=-1, keepdims=True) + EPS) * g


def _ada_kernel(c_ref, w_ref, b_ref, o_ref):
    c_act = _silu(c_ref[...])
    o_ref[0] = jnp.dot(c_act, w_ref[0], preferred_element_type=_F32,
                       precision=lax.Precision.HIGHEST) + b_ref[0]


def _ada(c, w_ada, b_ada):
    depth, d, n = w_ada.shape
    tn = 1024
    return pl.pallas_call(
        _ada_kernel,
        grid=(depth, n // tn),
        in_specs=[
            pl.BlockSpec(c.shape, lambda l, j: (0, 0)),
            pl.BlockSpec((1, d, tn), lambda l, j: (l, 0, j)),
            pl.BlockSpec((1, 1, tn), lambda l, j: (l, 0, j)),
        ],
        out_specs=pl.BlockSpec((1, c.shape[0], tn), lambda l, j: (l, 0, j)),
        out_shape=jax.ShapeDtypeStruct((depth, c.shape[0], n), _F32),
        name="ada_mod",
    )(c, w_ada, b_ada.reshape(depth, 1, n))


def _mixer_kernel(x_ref, sh_ref, sc_ref, gt_ref, ng_ref, win_ref, cw_ref, cb_ref, lng_ref, lnb_ref,
                  wco_ref, bco_ref, pw_ref, ps_ref, wout_ref, o_ref,
                  e_ref, s_ref, conv_ref, pe_ref, p2_ref, p4_ref, p8_ref):
    ts = SEQ_TILE
    j = pl.program_id(1)

    @pl.when(j == 0)
    def _():
        e_ref[0:HALO, :] = jnp.zeros((HALO, D_CONV), _F32)
        pe_ref[0:HALO, :] = jnp.zeros((HALO, D_POOL), _F32)

    @pl.when(j > 0)
    def _():
        e_ref[0:HALO, :] = e_ref[ts:ts + HALO, :]
        pe_ref[0:HALO, :] = pe_ref[ts:ts + HALO, :]

    xt = x_ref[0]
    h = _rms_norm(xt, ng_ref[...]) * (1.0 + sc_ref[0]) + sh_ref[0]
    hb = h.astype(_BF16)

    a = jnp.dot(hb, win_ref[:, 0:D_CONV], preferred_element_type=_F32)
    g = jnp.dot(hb, win_ref[:, D_CONV:2 * D_CONV], preferred_element_type=_F32)
    e_ref[HALO:HALO + ts, :] = a * _sigmoid(g)
    for sft in range(1, SUBLANES):
        s_ref[sft - 1] = e_ref[sft:sft + ts + HALO - SUBLANES, :]

    def conv_rows(i, carry):
        r0 = pl.multiple_of(i * CONV_ROWS, CONV_ROWS)
        for c in range(D_CONV // LANES):
            cs = slice(c * LANES, (c + 1) * LANES)
            acc = jnp.broadcast_to(cb_ref[:, cs], (CONV_ROWS, LANES))
            for k in range(CONV_K):
                q, sft = divmod(HALO - (CONV_K - 1) + k, SUBLANES)
                if sft == 0:
                    tap = e_ref[pl.ds(r0 + SUBLANES * q, CONV_ROWS), cs]
                else:
                    tap = s_ref[sft - 1, pl.ds(r0 + SUBLANES * q, CONV_ROWS), cs]
                acc = acc + tap * cw_ref[k:k + 1, cs]
            conv_ref[pl.ds(r0, CONV_ROWS), cs] = acc
        return carry

    lax.fori_loop(0, ts // CONV_ROWS, conv_rows, 0)

    v = conv_ref[...]
    mu = jnp.mean(v, axis=-1, keepdims=True)
    vc = v - mu
    var = jnp.mean(vc * vc, axis=-1, keepdims=True)
    v = _silu(vc * lax.rsqrt(var + EPS) * lng_ref[...] + lnb_ref[...])
    y_a = jnp.dot(v.astype(_BF16), wco_ref[...], preferred_element_type=_F32) + bco_ref[...]

    u = jnp.dot(hb, win_ref[:, COL_POOL:COL_GATE_A], preferred_element_type=_F32)
    pe_ref[HALO:HALO + ts, :] = u
    n_ext = ts + HALO
    p2_ref[8:n_ext, :] = pe_ref[8:n_ext, :] + pe_ref[7:n_ext - 1, :]
    p4_ref[16:n_ext, :] = p2_ref[16:n_ext, :] + p2_ref[14:n_ext - 2, :]
    p8_ref[24:n_ext, :] = p4_ref[24:n_ext, :] + p4_ref[20:n_ext - 4, :]
    pos = j * ts + lax.broadcasted_iota(jnp.int32, (ts, 1), 0) + 1
    y_b_parts = []
    for gi, w in enumerate(POOL_WINDOWS):
        cs = slice(gi * POOL_GROUP, (gi + 1) * POOL_GROUP)
        if w == 2:
            win_sum = p2_ref[HALO:n_ext, cs]
        elif w == 4:
            win_sum = p4_ref[HALO:n_ext, cs]
        elif w == 8:
            win_sum = p8_ref[HALO:n_ext, cs]
        else:
            win_sum = p8_ref[HALO:n_ext, cs] + p8_ref[HALO - 8:n_ext - 8, cs]
        count = jnp.minimum(pos, w).astype(_F32)
        pooled = win_sum / count - u[:, cs]
        y_b_parts.append(jnp.dot(pooled.astype(_BF16), pw_ref[gi], preferred_element_type=_F32))
    y_b = jnp.concatenate(y_b_parts, axis=-1) * ps_ref[...]

    gate_a = _sigmoid(jnp.dot(hb, win_ref[:, COL_GATE_A:COL_GATE_B], preferred_element_type=_F32))
    mixed = gate_a * y_a
    gate_b = _sigmoid(jnp.dot(hb, win_ref[:, COL_GATE_B:D_IN], preferred_element_type=_F32))
    mixed = mixed + gate_b * y_b
    out = jnp.dot(mixed.astype(_BF16), wout_ref[...], preferred_element_type=_F32)
    o_ref[0] = xt + gt_ref[0] * out


def _mixer(x, sh, sc, gt, norm_g, w_in, conv_w, conv_b, ln_g, ln_b, w_co, b_co, pool_w, pool_scale, w_out):
    b, s, d = x.shape
    ts = SEQ_TILE
    const2 = lambda bi, j: (0, 0)
    row = lambda a: a.reshape(1, -1)
    mod_spec = pl.BlockSpec((1, 1, d), lambda bi, j: (bi, 0, 0))
    return pl.pallas_call(
        _mixer_kernel,
        grid=(b, s // ts),
        in_specs=[
            pl.BlockSpec((1, ts, d), lambda bi, j: (bi, j, 0)),
            mod_spec, mod_spec, mod_spec,
            pl.BlockSpec((1, d), const2),
            pl.BlockSpec(w_in.shape, const2),
            pl.BlockSpec(conv_w.shape, const2),
            pl.BlockSpec((1, D_CONV), const2),
            pl.BlockSpec((1, D_CONV), const2),
            pl.BlockSpec((1, D_CONV), const2),
            pl.BlockSpec(w_co.shape, const2),
            pl.BlockSpec((1, d), const2),
            pl.BlockSpec(pool_w.shape, lambda bi, j: (0, 0, 0)),
            pl.BlockSpec((1, d), const2),
            pl.BlockSpec(w_out.shape, const2),
        ],
        out_specs=pl.BlockSpec((1, ts, d), lambda bi, j: (bi, j, 0)),
        out_shape=jax.ShapeDtypeStruct(x.shape, _F32),
        scratch_shapes=[
            pltpu.VMEM((ts + HALO, D_CONV), _F32),
            pltpu.VMEM((SUBLANES - 1, ts + HALO - SUBLANES, D_CONV), _F32),
            pltpu.VMEM((ts, D_CONV), _F32),
            pltpu.VMEM((ts + HALO, D_POOL), _F32),
            pltpu.VMEM((ts + HALO, D_POOL), _F32),
            pltpu.VMEM((ts + HALO, D_POOL), _F32),
            pltpu.VMEM((ts + HALO, D_POOL), _F32),
        ],
        compiler_params=pltpu.CompilerParams(
            dimension_semantics=("arbitrary", "arbitrary"), vmem_limit_bytes=VMEM_LIMIT),
        name="mixer",
    )(x, sh, sc, gt, row(norm_g), w_in, conv_w, row(conv_b), row(ln_g), row(ln_b), w_co, row(b_co),
      pool_w, row(pool_scale), w_out)


def _route_kernel(x_ref, sh_ref, sc_ref, ng_ref, wrt_ref, brt_ref, pos_ref, wgt_ref, seg_ref,
                  id_scr, rank_scr, wgt_scr, run_ref):
    tb = MOE_BLOCK
    j = pl.program_id(1)

    @pl.when(j == 0)
    def _():
        run_ref[...] = jnp.zeros_like(run_ref)

    h = _rms_norm(x_ref[...], ng_ref[...]) * (1.0 + sc_ref[0]) + sh_ref[0]
    logits = lax.dot_general(wrt_ref[...], h, (((1,), (1,)), ((), ())), preferred_element_type=_F32,
                             precision=lax.Precision.HIGHEST) + brt_ref[...]
    neg = jnp.float32(-jnp.inf)
    big = jnp.int32(N_EXPERTS)

    grow = lax.broadcasted_iota(jnp.int32, (SUBLANES, tb), 0)
    glog = jnp.where(grow < N_GROUPS, logits[GROUP_ROW0:GROUP_ROW0 + SUBLANES, :], neg)
    gmax = jnp.max(glog, axis=0, keepdims=True)
    p_group = 1.0 / jnp.sum(jnp.exp(glog - gmax), axis=0, keepdims=True)
    g_idx = jnp.min(jnp.where(glog == gmax, grow, big), axis=0, keepdims=True)

    erow = lax.broadcasted_iota(jnp.int32, (N_EXPERTS, tb), 0)
    el = jnp.where((erow // PER_GROUP) == g_idx, logits[0:N_EXPERTS, :], neg)
    m1 = jnp.max(el, axis=0, keepdims=True)
    i1 = jnp.min(jnp.where(el == m1, erow, big), axis=0, keepdims=True)
    el2 = jnp.where(erow == i1, neg, el)
    m2 = jnp.max(el2, axis=0, keepdims=True)
    i2 = jnp.min(jnp.where(el2 == m2, erow, big), axis=0, keepdims=True)
    r = jnp.exp(m2 - m1)
    w1 = p_group / (1.0 + r)
    w2 = w1 * r
    sel1 = erow == i1
    sel2 = erow == i2

    used = jnp.where(sel1 | sel2, 1.0, 0.0)
    tri_r = lax.broadcasted_iota(jnp.int32, (CUM_BLOCK, CUM_BLOCK), 0)
    tri_c = lax.broadcasted_iota(jnp.int32, (CUM_BLOCK, CUM_BLOCK), 1)
    upper = jnp.where(tri_r < tri_c, 1.0, 0.0).astype(_BF16)
    run = run_ref[:, 0:1]
    ranks = []
    for blk in range(tb // CUM_BLOCK):
        ub = used[:, blk * CUM_BLOCK:(blk + 1) * CUM_BLOCK]
        ranks.append(jnp.dot(ub.astype(_BF16), upper, preferred_element_type=_F32) + run)
        run = run + jnp.sum(ub, axis=1, keepdims=True)
    rank = jnp.concatenate(ranks, axis=1)
    run_ref[...] = jnp.broadcast_to(run, run_ref.shape)
    id_scr[j, 0:1, :] = i1
    id_scr[j, 1:2, :] = i2
    rank_scr[j, 0:1, :] = jnp.sum(jnp.where(sel1, rank, 0.0), axis=0, keepdims=True)
    rank_scr[j, 1:2, :] = jnp.sum(jnp.where(sel2, rank, 0.0), axis=0, keepdims=True)
    wgt_scr[j, 0:1, :] = w1
    wgt_scr[j, 1:2, :] = w2

    @pl.when(j == MOE_BLOCKS - 1)
    def _():
        cnt = jnp.concatenate([run_ref[...], jnp.zeros((LANES - N_EXPERTS, LANES), _F32)], axis=0)
        low_r = lax.broadcasted_iota(jnp.int32, (LANES, LANES), 0)
        low_c = lax.broadcasted_iota(jnp.int32, (LANES, LANES), 1)
        lower = jnp.where(low_c < low_r, 1.0, 0.0)
        off = jnp.dot(lower, cnt, preferred_element_type=_F32, precision=lax.Precision.HIGHEST)
        seg_ref[0, 0:N_EXPERTS, :] = off[0:N_EXPERTS, :].astype(jnp.int32)
        seg_ref[0, N_EXPERTS:2 * N_EXPERTS, :] = cnt[0:N_EXPERTS, :].astype(jnp.int32)
        off_col = off[0:N_EXPERTS, 0:1]
        for blk in range(MOE_BLOCKS):
            ls = slice(blk * tb, (blk + 1) * tb)
            for k in range(TOP_K):
                start = jnp.sum(jnp.where(erow == id_scr[blk, k:k + 1, :], off_col, 0.0),
                                axis=0, keepdims=True)
                slot = start + rank_scr[blk, k:k + 1, :]
                pos_ref[0, k:k + 1, ls] = (slot * ROW_VREGS).astype(jnp.int32)
                wgt_ref[0, k:k + 1, ls] = wgt_scr[blk, k:k + 1, :]


def _route(xf, sh, sc, norm_g, w_rt, b_rt, per_b):
    t, d = xf.shape
    tm, tb, nb = MOE_TILE, MOE_BLOCK, MOE_BLOCKS
    n_tiles = t // tm
    const2 = lambda i, j: (0, 0)
    mod_spec = pl.BlockSpec((1, 1, d), lambda i, j: (i // per_b, 0, 0))
    out3 = lambda i, j: (i, 0, 0)
    return pl.pallas_call(
        _route_kernel,
        grid=(n_tiles, nb),
        in_specs=[
            pl.BlockSpec((tb, d), lambda i, j: (i * nb + j, 0)),
            mod_spec, mod_spec,
            pl.BlockSpec((1, d), const2),
            pl.BlockSpec((ROUTER_ROWS, d), const2),
            pl.BlockSpec((ROUTER_ROWS, 1), const2),
        ],
        out_specs=[
            pl.BlockSpec((1, TOP_K, tm), out3),
            pl.BlockSpec((1, TOP_K, tm), out3),
            pl.BlockSpec((1, 2 * N_EXPERTS, LANES), out3),
        ],
        out_shape=[
            jax.ShapeDtypeStruct((n_tiles, TOP_K, tm), jnp.int32),
            jax.ShapeDtypeStruct((n_tiles, TOP_K, tm), _F32),
            jax.ShapeDtypeStruct((n_tiles, 2 * N_EXPERTS, LANES), jnp.int32),
        ],
        scratch_shapes=[
            pltpu.VMEM((nb, TOP_K, tb), jnp.int32),
            pltpu.VMEM((nb, TOP_K, tb), _F32),
            pltpu.VMEM((nb, TOP_K, tb), _F32),
            pltpu.VMEM((N_EXPERTS, LANES), _F32),
        ],
        compiler_params=pltpu.CompilerParams(
            dimension_semantics=("arbitrary", "arbitrary"), vmem_limit_bytes=VMEM_LIMIT),
        name="route",
    )(xf, sh, sc, norm_g.reshape(1, d), w_rt, b_rt)


def _row_ds(offset):
    return pl.ds(pl.multiple_of(offset, ROW_VREGS), ROW_VREGS)


def _moe_kernel(pos_s, wgt_s, seg_s,
                x_ref, sh_ref, sc_ref, gt_ref, ng_ref, fg_ref, wgu_hbm, wd_hbm, o_ref,
                tm_ref, srt_ref, wgu_buf, wd_buf, sem, *, final_norm):
    tm, tb, nb, ch = MOE_TILE, MOE_BLOCK, MOE_BLOCKS, MOE_CHUNK
    i = pl.program_id(0)
    j = pl.program_id(1)
    base = i * (TOP_K * tm) + (j % nb) * tb

    def weight_copies(e, slot):
        return (pltpu.make_async_copy(wgu_hbm.at[e], wgu_buf.at[slot], sem.at[0, slot]),
                pltpu.make_async_copy(wd_hbm.at[e], wd_buf.at[slot], sem.at[1, slot]))

    @pl.when(j == 0)
    def _():
        for cp in weight_copies(0, 0):
            cp.start()
        srt_ref[TOP_K * tm * ROW_VREGS:, :] = jnp.zeros((ch * ROW_VREGS, LANES), _F32)

    @pl.when(j < nb)
    def _():
        h = _rms_norm(x_ref[...], ng_ref[...]) * (1.0 + sc_ref[0]) + sh_ref[0]
        for s in range(ROW_VREGS):
            tm_ref[pl.ds(s, tb, stride=ROW_VREGS), :] = h[:, s * LANES:(s + 1) * LANES]

        def dispatch(it, carry):
            t0 = it * UNROLL
            rows = [tm_ref[_row_ds((t0 + u) * ROW_VREGS), :] for u in range(UNROLL)]
            for u in range(UNROLL):
                for k in range(TOP_K):
                    srt_ref[_row_ds(pos_s[base + k * tm + t0 + u]), :] = rows[u]
            return carry

        lax.fori_loop(0, tb // UNROLL, dispatch, 0)

    @pl.when(j == nb)
    def _():
        def expert(e, carry):
            slot = e % 2
            for cp in weight_copies(e, slot):
                cp.wait()

            @pl.when(e + 1 < N_EXPERTS)
            def _():
                for cp in weight_copies(e + 1, 1 - slot):
                    cp.start()

            off = seg_s[i * (2 * N_EXPERTS) + e]
            cnt = seg_s[i * (2 * N_EXPERTS) + N_EXPERTS + e]

            def chunk(c, inner):
                r0 = (off + c * ch) * ROW_VREGS

                def rows_of(s):
                    return srt_ref[pl.ds(r0 + s, ch, stride=ROW_VREGS), :]

                xs = jnp.concatenate([rows_of(s) for s in range(ROW_VREGS)], axis=1)
                hgu = jnp.dot(xs.astype(_BF16), wgu_buf[slot], preferred_element_type=_F32)
                act = _silu(hgu[:, :D_EXPERT]) * hgu[:, D_EXPERT:]
                y = jnp.dot(act.astype(_BF16), wd_buf[slot], preferred_element_type=_F32)
                valid = lax.broadcasted_iota(jnp.int32, (ch, 1), 0) < (cnt - c * ch)
                for s in range(ROW_VREGS):
                    srt_ref[pl.ds(r0 + s, ch, stride=ROW_VREGS), :] = jnp.where(
                        valid, y[:, s * LANES:(s + 1) * LANES], rows_of(s))
                return inner

            lax.fori_loop(0, (cnt + ch - 1) // ch, chunk, 0)
            return carry

        lax.fori_loop(0, N_EXPERTS, expert, 0)

    @pl.when(j >= nb)
    def _():
        def combine(it, carry):
            t0 = it * UNROLL
            outs = []
            for u in range(UNROLL):
                t = base + t0 + u
                y0 = srt_ref[_row_ds(pos_s[t]), :]
                y1 = srt_ref[_row_ds(pos_s[t + tm]), :]
                outs.append(wgt_s[t] * y0 + wgt_s[t + tm] * y1)
            for u in range(UNROLL):
                tm_ref[_row_ds((t0 + u) * ROW_VREGS), :] = outs[u]
            return carry

        lax.fori_loop(0, tb // UNROLL, combine, 0)

        moe = jnp.concatenate(
            [tm_ref[pl.ds(s, tb, stride=ROW_VREGS), :] for s in range(ROW_VREGS)], axis=1)
        y = x_ref[...] + gt_ref[0] * moe
        if final_norm:
            y = _rms_norm(y, fg_ref[...])
        o_ref[...] = y


def _moe(x, sh, sc, gt, norm_g, w_rt, b_rt, w_gu, w_d, final_g, final_norm):
    b, s, d = x.shape
    t = b * s
    tm, tb, nb = MOE_TILE, MOE_BLOCK, MOE_BLOCKS
    per_b = s // tm
    n_tiles = t // tm
    xf = x.reshape(t, d)
    pos, wgt, seg = _route(xf, sh, sc, norm_g, w_rt, b_rt, per_b)
    const2 = lambda i, j, *_: (0, 0)
    mod_spec = pl.BlockSpec((1, 1, d), lambda i, j, *_: (i // per_b, 0, 0))
    out = pl.pallas_call(
        functools.partial(_moe_kernel, final_norm=final_norm),
        grid_spec=pltpu.PrefetchScalarGridSpec(
            num_scalar_prefetch=3,
            grid=(n_tiles, 2 * nb),
            in_specs=[
                pl.BlockSpec((tb, d), lambda i, j, *_: (i * nb + j % nb, 0)),
                mod_spec, mod_spec, mod_spec,
                pl.BlockSpec((1, d), const2),
                pl.BlockSpec((1, d), const2),
                pl.BlockSpec(memory_space=pl.ANY),
                pl.BlockSpec(memory_space=pl.ANY),
            ],
            out_specs=pl.BlockSpec((tb, d), lambda i, j, *_: (i * nb + jnp.maximum(j - nb, 0), 0)),
            scratch_shapes=[
                pltpu.VMEM((tb * ROW_VREGS, LANES), _F32),
                pltpu.VMEM(((TOP_K * tm + MOE_CHUNK) * ROW_VREGS, LANES), _F32),
                pltpu.VMEM((2, d, 2 * D_EXPERT), _BF16),
                pltpu.VMEM((2, D_EXPERT, d), _BF16),
                pltpu.SemaphoreType.DMA((2, 2)),
            ],
        ),
        out_shape=jax.ShapeDtypeStruct((t, d), _F32),
        compiler_params=pltpu.CompilerParams(
            dimension_semantics=("arbitrary", "arbitrary"), vmem_limit_bytes=VMEM_LIMIT),
        name="moe",
    )(pos.reshape(-1), wgt.reshape(-1), seg[:, :, 0].reshape(-1),
      xf, sh, sc, gt, norm_g.reshape(1, d), final_g.reshape(1, d), w_gu, w_d)
    return out.reshape(b, s, d)


def _router_params(w_rg, b_rg, w_re, b_re):
    d = w_rg.shape[0]
    pad = ROUTER_ROWS - N_EXPERTS - N_GROUPS
    w = jnp.concatenate([w_re.T, w_rg.T, jnp.zeros((pad, d), _F32)], axis=0)
    b = jnp.concatenate([b_re, b_rg, jnp.zeros((pad,), _F32)]).reshape(ROUTER_ROWS, 1)
    return w, b


def kernel(x, c, mixer_norm_g, w_ada, b_ada, w_in, conv_w, conv_b, conv_ln_g, conv_ln_b, w_conv_out,
           b_conv_out, pool_w, pool_scale, w_out, ffn_norm_g, w_router_group, b_router_group,
           w_router_expert, b_router_expert, w_expert_gate, w_expert_up, w_expert_down, final_norm_g):
    depth = w_ada.shape[0]
    bsz = x.shape[0]
    mod = _ada(c, w_ada, b_ada)
    mod = mod.reshape(depth, bsz, N_MOD, 1, D_MODEL)
    for l in range(depth):
        sh1, sc1, g1, sh2, sc2, g2 = (mod[l, :, i] for i in range(N_MOD))
        x = _mixer(x, sh1, sc1, g1, mixer_norm_g[l], w_in[l].astype(_BF16), conv_w[l], conv_b[l],
                   conv_ln_g[l], conv_ln_b[l], w_conv_out[l].astype(_BF16), b_conv_out[l],
                   pool_w[l].astype(_BF16), pool_scale[l], w_out[l].astype(_BF16))
        w_rt, b_rt = _router_params(w_router_group[l], b_router_group[l],
                                    w_router_expert[l], b_router_expert[l])
        w_gu = jnp.concatenate([w_expert_gate[l], w_expert_up[l]], axis=-1).astype(_BF16)
        x = _moe(x, sh2, sc2, g2, ffn_norm_g[l], w_rt, b_rt, w_gu, w_expert_down[l].astype(_BF16),
                 final_norm_g, final_norm=(l == depth - 1))
    return x
```

```python
import functools

import jax
import jax.numpy as jnp
from jax import lax
from jax.experimental import pallas as pl
from jax.experimental.pallas import tpu as pltpu

D_MODEL = 1024
D_CONV = 512
D_POOL = 512
CONV_K = 31
POOL_WINDOWS = (2, 4, 8, 16)
POOL_GROUP = 128
POOL_OUT_GROUP = 256
N_GROUPS = 4
PER_GROUP = 8
N_EXPERTS = 32
TOP_K = 2
D_EXPERT = 256
N_MOD = 6
EPS = 1e-6

COL_POOL = 2 * D_CONV
COL_GATE_A = COL_POOL + D_POOL
COL_GATE_B = COL_GATE_A + D_MODEL
D_IN = COL_GATE_B + D_MODEL

LANES = 128
SUBLANES = 8
ROW_VREGS = D_MODEL // LANES
HALO = 32
SEQ_TILE = 512
CONV_ROWS = 128
CONV_ROW_STEPS = SEQ_TILE // CONV_ROWS
CONV_LANE_GROUPS = D_CONV // LANES
CONV_ACCS = 2
GATE_COLS = 256
GATE_COL_BLOCKS = (D_IN - COL_GATE_A) // GATE_COLS
MOE_TILE = 4096
MOE_BLOCK = 512
MOE_BLOCKS = MOE_TILE // MOE_BLOCK
MOE_CHUNK = 256
MOE_SUB = 128
CUM_BLOCK = 256
ROUTER_ROWS = 40
GROUP_ROW0 = N_EXPERTS
UNROLL = 8
VMEM_LIMIT = 56 * 1024 * 1024

_F32 = jnp.float32
_BF16 = jnp.bfloat16
assert ROW_VREGS == SUBLANES


def _sigmoid(v):
    return 1.0 / (1.0 + jnp.exp(-v))


def _silu(v):
    return v * _sigmoid(v)


def _rms_norm(v, g):
    return v * lax.rsqrt(jnp.mean(v * v, axis=-1, keepdims=True) + EPS) * g


def _ada_kernel(c_ref, w_ref, b_ref, o_ref):
    c_act = _silu(c_ref[...])
    o_ref[0] = jnp.dot(c_act, w_ref[0], preferred_element_type=_F32,
                       precision=lax.Precision.HIGHEST) + b_ref[0]


def _ada(c, w_ada, b_ada):
    depth, d, n = w_ada.shape
    tn = 1024
    return pl.pallas_call(
        _ada_kernel,
        grid=(depth, n // tn),
        in_specs=[
            pl.BlockSpec(c.shape, lambda l, j: (0, 0)),
            pl.BlockSpec((1, d, tn), lambda l, j: (l, 0, j)),
            pl.BlockSpec((1, 1, tn), lambda l, j: (l, 0, j)),
        ],
        out_specs=pl.BlockSpec((1, c.shape[0], tn), lambda l, j: (l, 0, j)),
        out_shape=jax.ShapeDtypeStruct((depth, c.shape[0], n), _F32),
        name="ada_mod",
    )(c, w_ada, b_ada.reshape(depth, 1, n))


def _mixer_kernel(x_ref, sh_ref, sc_ref, gt_ref, ng_ref, win_ref, wg_ref, cw_ref, cb_ref, lng_ref, lnb_ref,
                  wco_ref, bco_ref, pw_ref, ps_ref, wout_ref, o_ref,
                  gates_ref, e_ref, s_ref, conv_ref, pe_ref, p2_ref, p4_ref, p8_ref):
    ts = SEQ_TILE
    j = pl.program_id(1)

    @pl.when(j == 0)
    def _():
        e_ref[:, 0:HALO, :] = jnp.zeros((CONV_LANE_GROUPS, HALO, LANES), _F32)
        pe_ref[0:HALO, :] = jnp.zeros((HALO, D_POOL), _F32)

    @pl.when(j > 0)
    def _():
        e_ref[:, 0:HALO, :] = e_ref[:, ts:ts + HALO, :]
        pe_ref[0:HALO, :] = pe_ref[ts:ts + HALO, :]

    xt = x_ref[0]
    h = _rms_norm(xt, ng_ref[...]) * (1.0 + sc_ref[0]) + sh_ref[0]
    hb = h.astype(_BF16)

    a = jnp.dot(hb, win_ref[:, 0:D_CONV], preferred_element_type=_F32)
    g = jnp.dot(hb, win_ref[:, D_CONV:2 * D_CONV], preferred_element_type=_F32)
    glu = a * _sigmoid(g)
    for c in range(CONV_LANE_GROUPS):
        e_ref[c, HALO:HALO + ts, :] = glu[:, c * LANES:(c + 1) * LANES]
        for sft in range(1, SUBLANES):
            s_ref[sft - 1, c] = e_ref[c, sft:sft + ts + HALO - SUBLANES, :]

    def conv_block(c, r0):
        accs = [jnp.broadcast_to(cb_ref[c], (CONV_ROWS, LANES))] + [None] * (CONV_ACCS - 1)
        for k in range(CONV_K):
            q, sft = divmod(HALO - (CONV_K - 1) + k, SUBLANES)
            lo = r0 + SUBLANES * q
            tap = e_ref[c, lo:lo + CONV_ROWS, :] if sft == 0 else s_ref[sft - 1, c, lo:lo + CONV_ROWS, :]
            prod = tap * cw_ref[k, c]
            a_i = k % CONV_ACCS
            accs[a_i] = prod if accs[a_i] is None else accs[a_i] + prod
        while len(accs) > 1:
            accs = [accs[p] + accs[p + 1] for p in range(0, len(accs), 2)]
        conv_ref[c, r0:r0 + CONV_ROWS, :] = accs[0]

    n_conv = CONV_LANE_GROUPS * CONV_ROW_STEPS
    convs_per_gate = n_conv // GATE_COL_BLOCKS
    for it in range(n_conv):
        if it % convs_per_gate == 0:
            n = it // convs_per_gate
            gates_ref[n] = _sigmoid(jnp.dot(hb, wg_ref[n], preferred_element_type=_F32))
        conv_block(it // CONV_ROW_STEPS, (it % CONV_ROW_STEPS) * CONV_ROWS)

    v = jnp.concatenate([conv_ref[c] for c in range(CONV_LANE_GROUPS)], axis=1)
    mu = jnp.mean(v, axis=-1, keepdims=True)
    vc = v - mu
    var = jnp.mean(vc * vc, axis=-1, keepdims=True)
    v = _silu(vc * lax.rsqrt(var + EPS) * lng_ref[...] + lnb_ref[...])
    y_a = jnp.dot(v.astype(_BF16), wco_ref[...], preferred_element_type=_F32) + bco_ref[...]

    u = jnp.dot(hb, win_ref[:, COL_POOL:COL_GATE_A], preferred_element_type=_F32)
    pe_ref[HALO:HALO + ts, :] = u
    n_ext = ts + HALO
    p2_ref[8:n_ext, :] = pe_ref[8:n_ext, :] + pe_ref[7:n_ext - 1, :]
    p4_ref[16:n_ext, :] = p2_ref[16:n_ext, :] + p2_ref[14:n_ext - 2, :]
    p8_ref[24:n_ext, :] = p4_ref[24:n_ext, :] + p4_ref[20:n_ext - 4, :]
    pos = j * ts + lax.broadcasted_iota(jnp.int32, (ts, 1), 0) + 1
    y_b_parts = []
    for gi, w in enumerate(POOL_WINDOWS):
        cs = slice(gi * POOL_GROUP, (gi + 1) * POOL_GROUP)
        if w == 2:
            win_sum = p2_ref[HALO:n_ext, cs]
        elif w == 4:
            win_sum = p4_ref[HALO:n_ext, cs]
        elif w == 8:
            win_sum = p8_ref[HALO:n_ext, cs]
        else:
            win_sum = p8_ref[HALO:n_ext, cs] + p8_ref[HALO - 8:n_ext - 8, cs]
        count = jnp.minimum(pos, w).astype(_F32)
        pooled = win_sum / count - u[:, cs]
        y_b_parts.append(jnp.dot(pooled.astype(_BF16), pw_ref[gi], preferred_element_type=_F32))
    y_b = jnp.concatenate(y_b_parts, axis=-1) * ps_ref[...]

    half = GATE_COL_BLOCKS // 2
    gate_a = jnp.concatenate([gates_ref[n] for n in range(half)], axis=1)
    gate_b = jnp.concatenate([gates_ref[n] for n in range(half, GATE_COL_BLOCKS)], axis=1)
    mixed = gate_a * y_a + gate_b * y_b
    out = jnp.dot(mixed.astype(_BF16), wout_ref[...], preferred_element_type=_F32)
    o_ref[0] = xt + gt_ref[0] * out


def _mixer(x, sh, sc, gt, norm_g, w_in, conv_w, conv_b, ln_g, ln_b, w_co, b_co, pool_w, pool_scale, w_out):
    b, s, d = x.shape
    ts = SEQ_TILE
    const2 = lambda bi, j: (0, 0)
    const3 = lambda bi, j: (0, 0, 0)
    row = lambda a: a.reshape(1, -1)
    mod_spec = pl.BlockSpec((1, 1, d), lambda bi, j: (bi, 0, 0))
    w_main = w_in[:, :COL_GATE_A]
    w_gates = w_in[:, COL_GATE_A:].reshape(d, GATE_COL_BLOCKS, GATE_COLS).transpose(1, 0, 2)
    conv_w = conv_w.reshape(CONV_K, CONV_LANE_GROUPS, 1, LANES)
    conv_b = conv_b.reshape(CONV_LANE_GROUPS, 1, LANES)
    return pl.pallas_call(
        _mixer_kernel,
        grid=(b, s // ts),
        in_specs=[
            pl.BlockSpec((1, ts, d), lambda bi, j: (bi, j, 0)),
            mod_spec, mod_spec, mod_spec,
            pl.BlockSpec((1, d), const2),
            pl.BlockSpec(w_main.shape, const2),
            pl.BlockSpec(w_gates.shape, const3),
            pl.BlockSpec(conv_w.shape, lambda bi, j: (0, 0, 0, 0)),
            pl.BlockSpec(conv_b.shape, const3),
            pl.BlockSpec((1, D_CONV), const2),
            pl.BlockSpec((1, D_CONV), const2),
            pl.BlockSpec(w_co.shape, const2),
            pl.BlockSpec((1, d), const2),
            pl.BlockSpec(pool_w.shape, lambda bi, j: (0, 0, 0)),
            pl.BlockSpec((1, d), const2),
            pl.BlockSpec(w_out.shape, const2),
        ],
        out_specs=pl.BlockSpec((1, ts, d), lambda bi, j: (bi, j, 0)),
        out_shape=jax.ShapeDtypeStruct(x.shape, _F32),
        scratch_shapes=[
            pltpu.VMEM((GATE_COL_BLOCKS, ts, GATE_COLS), _F32),
            pltpu.VMEM((CONV_LANE_GROUPS, ts + HALO, LANES), _F32),
            pltpu.VMEM((SUBLANES - 1, CONV_LANE_GROUPS, ts + HALO - SUBLANES, LANES), _F32),
            pltpu.VMEM((CONV_LANE_GROUPS, ts, LANES), _F32),
            pltpu.VMEM((ts + HALO, D_POOL), _F32),
            pltpu.VMEM((ts + HALO, D_POOL), _F32),
            pltpu.VMEM((ts + HALO, D_POOL), _F32),
            pltpu.VMEM((ts + HALO, D_POOL), _F32),
        ],
        compiler_params=pltpu.CompilerParams(
            dimension_semantics=("arbitrary", "arbitrary"), vmem_limit_bytes=VMEM_LIMIT),
        name="mixer",
    )(x, sh, sc, gt, row(norm_g), w_main, w_gates, conv_w, conv_b, row(ln_g), row(ln_b), w_co, row(b_co),
      pool_w, row(pool_scale), w_out)


def _route_kernel(x_ref, sh_ref, sc_ref, ng_ref, wrt_ref, brt_ref, pos_ref, wgt_ref, seg_ref,
                  id_scr, rank_scr, wgt_scr, run_ref):
    tb = MOE_BLOCK
    j = pl.program_id(1)

    @pl.when(j == 0)
    def _():
        run_ref[...] = jnp.zeros_like(run_ref)

    h = _rms_norm(x_ref[...], ng_ref[...]) * (1.0 + sc_ref[0]) + sh_ref[0]
    logits = lax.dot_general(wrt_ref[...], h, (((1,), (1,)), ((), ())), preferred_element_type=_F32,
                             precision=lax.Precision.HIGHEST) + brt_ref[...]
    neg = jnp.float32(-jnp.inf)
    big = jnp.int32(N_EXPERTS)

    grow = lax.broadcasted_iota(jnp.int32, (SUBLANES, tb), 0)
    glog = jnp.where(grow < N_GROUPS, logits[GROUP_ROW0:GROUP_ROW0 + SUBLANES, :], neg)
    gmax = jnp.max(glog, axis=0, keepdims=True)
    p_group = 1.0 / jnp.sum(jnp.exp(glog - gmax), axis=0, keepdims=True)
    g_idx = jnp.min(jnp.where(glog == gmax, grow, big), axis=0, keepdims=True)

    erow = lax.broadcasted_iota(jnp.int32, (N_EXPERTS, tb), 0)
    el = jnp.where((erow // PER_GROUP) == g_idx, logits[0:N_EXPERTS, :], neg)
    m1 = jnp.max(el, axis=0, keepdims=True)
    i1 = jnp.min(jnp.where(el == m1, erow, big), axis=0, keepdims=True)
    el2 = jnp.where(erow == i1, neg, el)
    m2 = jnp.max(el2, axis=0, keepdims=True)
    i2 = jnp.min(jnp.where(el2 == m2, erow, big), axis=0, keepdims=True)
    r = jnp.exp(m2 - m1)
    w1 = p_group / (1.0 + r)
    w2 = w1 * r
    sel1 = erow == i1
    sel2 = erow == i2

    used = jnp.where(sel1 | sel2, 1.0, 0.0)
    tri_r = lax.broadcasted_iota(jnp.int32, (CUM_BLOCK, CUM_BLOCK), 0)
    tri_c = lax.broadcasted_iota(jnp.int32, (CUM_BLOCK, CUM_BLOCK), 1)
    upper = jnp.where(tri_r < tri_c, 1.0, 0.0).astype(_BF16)
    run = run_ref[:, 0:1]
    ranks = []
    for blk in range(tb // CUM_BLOCK):
        ub = used[:, blk * CUM_BLOCK:(blk + 1) * CUM_BLOCK]
        ranks.append(jnp.dot(ub.astype(_BF16), upper, preferred_element_type=_F32) + run)
        run = run + jnp.sum(ub, axis=1, keepdims=True)
    rank = jnp.concatenate(ranks, axis=1)
    run_ref[...] = jnp.broadcast_to(run, run_ref.shape)
    id_scr[j, 0:1, :] = i1
    id_scr[j, 1:2, :] = i2
    rank_scr[j, 0:1, :] = jnp.sum(jnp.where(sel1, rank, 0.0), axis=0, keepdims=True)
    rank_scr[j, 1:2, :] = jnp.sum(jnp.where(sel2, rank, 0.0), axis=0, keepdims=True)
    wgt_scr[j, 0:1, :] = w1
    wgt_scr[j, 1:2, :] = w2

    @pl.when(j == MOE_BLOCKS - 1)
    def _():
        cnt = jnp.concatenate([run_ref[...], jnp.zeros((LANES - N_EXPERTS, LANES), _F32)], axis=0)
        low_r = lax.broadcasted_iota(jnp.int32, (LANES, LANES), 0)
        low_c = lax.broadcasted_iota(jnp.int32, (LANES, LANES), 1)
        lower = jnp.where(low_c < low_r, 1.0, 0.0)
        off = jnp.dot(lower, cnt, preferred_element_type=_F32, precision=lax.Precision.HIGHEST)
        seg_ref[0, 0:N_EXPERTS, :] = off[0:N_EXPERTS, :].astype(jnp.int32)
        seg_ref[0, N_EXPERTS:2 * N_EXPERTS, :] = cnt[0:N_EXPERTS, :].astype(jnp.int32)
        off_col = off[0:N_EXPERTS, 0:1]
        for blk in range(MOE_BLOCKS):
            ls = slice(blk * tb, (blk + 1) * tb)
            for k in range(TOP_K):
                start = jnp.sum(jnp.where(erow == id_scr[blk, k:k + 1, :], off_col, 0.0),
                                axis=0, keepdims=True)
                slot = start + rank_scr[blk, k:k + 1, :]
                pos_ref[0, k:k + 1, ls] = (slot * ROW_VREGS).astype(jnp.int32)
                wgt_ref[0, k:k + 1, ls] = wgt_scr[blk, k:k + 1, :]


def _route(xf, sh, sc, norm_g, w_rt, b_rt, per_b):
    t, d = xf.shape
    tm, tb, nb = MOE_TILE, MOE_BLOCK, MOE_BLOCKS
    n_tiles = t // tm
    const2 = lambda i, j: (0, 0)
    mod_spec = pl.BlockSpec((1, 1, d), lambda i, j: (i // per_b, 0, 0))
    out3 = lambda i, j: (i, 0, 0)
    return pl.pallas_call(
        _route_kernel,
        grid=(n_tiles, nb),
        in_specs=[
            pl.BlockSpec((tb, d), lambda i, j: (i * nb + j, 0)),
            mod_spec, mod_spec,
            pl.BlockSpec((1, d), const2),
            pl.BlockSpec((ROUTER_ROWS, d), const2),
            pl.BlockSpec((ROUTER_ROWS, 1), const2),
        ],
        out_specs=[
            pl.BlockSpec((1, TOP_K, tm), out3),
            pl.BlockSpec((1, TOP_K, tm), out3),
            pl.BlockSpec((1, 2 * N_EXPERTS, LANES), out3),
        ],
        out_shape=[
            jax.ShapeDtypeStruct((n_tiles, TOP_K, tm), jnp.int32),
            jax.ShapeDtypeStruct((n_tiles, TOP_K, tm), _F32),
            jax.ShapeDtypeStruct((n_tiles, 2 * N_EXPERTS, LANES), jnp.int32),
        ],
        scratch_shapes=[
            pltpu.VMEM((nb, TOP_K, tb), jnp.int32),
            pltpu.VMEM((nb, TOP_K, tb), _F32),
            pltpu.VMEM((nb, TOP_K, tb), _F32),
            pltpu.VMEM((N_EXPERTS, LANES), _F32),
        ],
        compiler_params=pltpu.CompilerParams(
            dimension_semantics=("arbitrary", "arbitrary"), vmem_limit_bytes=VMEM_LIMIT),
        name="route",
    )(xf, sh, sc, norm_g.reshape(1, d), w_rt, b_rt)


def _row_ds(offset):
    return pl.ds(pl.multiple_of(offset, ROW_VREGS), ROW_VREGS)


def _moe_kernel(pos_s, wgt_s, seg_s,
                x_ref, sh_ref, sc_ref, gt_ref, ng_ref, fg_ref, wgu_hbm, wd_hbm, o_ref,
                tm_ref, srt_ref, wgu_buf, wd_buf, sem, *, final_norm):
    tm, tb, nb, ch = MOE_TILE, MOE_BLOCK, MOE_BLOCKS, MOE_CHUNK
    i = pl.program_id(0)
    j = pl.program_id(1)
    base = i * (TOP_K * tm) + (j % nb) * tb

    def weight_copies(e, slot):
        return (pltpu.make_async_copy(wgu_hbm.at[e], wgu_buf.at[slot], sem.at[0, slot]),
                pltpu.make_async_copy(wd_hbm.at[e], wd_buf.at[slot], sem.at[1, slot]))

    @pl.when(j == 0)
    def _():
        for cp in weight_copies(0, 0):
            cp.start()
        srt_ref[TOP_K * tm * ROW_VREGS:, :] = jnp.zeros((ch * ROW_VREGS, LANES), _F32)

    @pl.when(j < nb)
    def _():
        h = _rms_norm(x_ref[...], ng_ref[...]) * (1.0 + sc_ref[0]) + sh_ref[0]
        for s in range(ROW_VREGS):
            tm_ref[pl.ds(s, tb, stride=ROW_VREGS), :] = h[:, s * LANES:(s + 1) * LANES]

        def dispatch(it, carry):
            t0 = it * UNROLL
            rows = [tm_ref[_row_ds((t0 + u) * ROW_VREGS), :] for u in range(UNROLL)]
            for u in range(UNROLL):
                for k in range(TOP_K):
                    srt_ref[_row_ds(pos_s[base + k * tm + t0 + u]), :] = rows[u]
            return carry

        lax.fori_loop(0, tb // UNROLL, dispatch, 0)

    @pl.when(j == nb)
    def _():
        def expert(e, carry):
            slot = e % 2
            for cp in weight_copies(e, slot):
                cp.wait()

            @pl.when(e + 1 < N_EXPERTS)
            def _():
                for cp in weight_copies(e + 1, 1 - slot):
                    cp.start()

            off = seg_s[i * (2 * N_EXPERTS) + e]
            cnt = seg_s[i * (2 * N_EXPERTS) + N_EXPERTS + e]

            def chunk(c, inner):
                subs = range(ch // MOE_SUB)
                row0 = [c * ch + sub * MOE_SUB for sub in subs]
                r0 = [(off + r) * ROW_VREGS for r in row0]
                xs = [jnp.concatenate([srt_ref[pl.ds(r0[sub] + s, MOE_SUB, stride=ROW_VREGS), :]
                                       for s in range(ROW_VREGS)], axis=1) for sub in subs]
                ys = []
                hgus = [jnp.dot(xs[sub].astype(_BF16), wgu_buf[slot], preferred_element_type=_F32)
                        for sub in subs]
                for sub in subs:
                    hgu = hgus[sub]
                    act = _silu(hgu[:, :D_EXPERT]) * hgu[:, D_EXPERT:]
                    y = jnp.dot(act.astype(_BF16), wd_buf[slot], preferred_element_type=_F32)
                    valid = lax.broadcasted_iota(jnp.int32, (MOE_SUB, 1), 0) < (cnt - row0[sub])
                    ys.append(jnp.where(valid, y, xs[sub]))
                for sub in subs:
                    for s in range(ROW_VREGS):
                        srt_ref[pl.ds(r0[sub] + s, MOE_SUB, stride=ROW_VREGS), :] = (
                            ys[sub][:, s * LANES:(s + 1) * LANES])
                return inner

            lax.fori_loop(0, (cnt + ch - 1) // ch, chunk, 0)
            return carry

        lax.fori_loop(0, N_EXPERTS, expert, 0)

    @pl.when(j >= nb)
    def _():
        def combine(it, carry):
            t0 = it * UNROLL
            outs = []
            for u in range(UNROLL):
                t = base + t0 + u
                y0 = srt_ref[_row_ds(pos_s[t]), :]
                y1 = srt_ref[_row_ds(pos_s[t + tm]), :]
                outs.append(wgt_s[t] * y0 + wgt_s[t + tm] * y1)
            for u in range(UNROLL):
                tm_ref[_row_ds((t0 + u) * ROW_VREGS), :] = outs[u]
            return carry

        lax.fori_loop(0, tb // UNROLL, combine, 0)

        moe = jnp.concatenate(
            [tm_ref[pl.ds(s, tb, stride=ROW_VREGS), :] for s in range(ROW_VREGS)], axis=1)
        y = x_ref[...] + gt_ref[0] * moe
        if final_norm:
            y = _rms_norm(y, fg_ref[...])
        o_ref[...] = y


def _moe(x, sh, sc, gt, norm_g, w_rt, b_rt, w_gu, w_d, final_g, final_norm):
    b, s, d = x.shape
    t = b * s
    tm, tb, nb = MOE_TILE, MOE_BLOCK, MOE_BLOCKS
    per_b = s // tm
    n_tiles = t // tm
    xf = x.reshape(t, d)
    pos, wgt, seg = _route(xf, sh, sc, norm_g, w_rt, b_rt, per_b)
    const2 = lambda i, j, *_: (0, 0)
    mod_spec = pl.BlockSpec((1, 1, d), lambda i, j, *_: (i // per_b, 0, 0))
    out = pl.pallas_call(
        functools.partial(_moe_kernel, final_norm=final_norm),
        grid_spec=pltpu.PrefetchScalarGridSpec(
            num_scalar_prefetch=3,
            grid=(n_tiles, 2 * nb),
            in_specs=[
                pl.BlockSpec((tb, d), lambda i, j, *_: (i * nb + j % nb, 0)),
                mod_spec, mod_spec, mod_spec,
                pl.BlockSpec((1, d), const2),
                pl.BlockSpec((1, d), const2),
                pl.BlockSpec(memory_space=pl.ANY),
                pl.BlockSpec(memory_space=pl.ANY),
            ],
            out_specs=pl.BlockSpec((tb, d), lambda i, j, *_: (i * nb + jnp.maximum(j - nb, 0), 0)),
            scratch_shapes=[
                pltpu.VMEM((tb * ROW_VREGS, LANES), _F32),
                pltpu.VMEM(((TOP_K * tm + MOE_CHUNK) * ROW_VREGS, LANES), _F32),
                pltpu.VMEM((2, d, 2 * D_EXPERT), _BF16),
                pltpu.VMEM((2, D_EXPERT, d), _BF16),
                pltpu.SemaphoreType.DMA((2, 2)),
            ],
        ),
        out_shape=jax.ShapeDtypeStruct((t, d), _F32),
        compiler_params=pltpu.CompilerParams(
            dimension_semantics=("arbitrary", "arbitrary"), vmem_limit_bytes=VMEM_LIMIT),
        name="moe",
    )(pos.reshape(-1), wgt.reshape(-1), seg[:, :, 0].reshape(-1),
      xf, sh, sc, gt, norm_g.reshape(1, d), final_g.reshape(1, d), w_gu, w_d)
    return out.reshape(b, s, d)


def _router_params(w_rg, b_rg, w_re, b_re):
    d = w_rg.shape[0]
    pad = ROUTER_ROWS - N_EXPERTS - N_GROUPS
    w = jnp.concatenate([w_re.T, w_rg.T, jnp.zeros((pad, d), _F32)], axis=0)
    b = jnp.concatenate([b_re, b_rg, jnp.zeros((pad,), _F32)]).reshape(ROUTER_ROWS, 1)
    return w, b


def kernel(x, c, mixer_norm_g, w_ada, b_ada, w_in, conv_w, conv_b, conv_ln_g, conv_ln_b, w_conv_out,
           b_conv_out, pool_w, pool_scale, w_out, ffn_norm_g, w_router_group, b_router_group,
           w_router_expert, b_router_expert, w_expert_gate, w_expert_up, w_expert_down, final_norm_g):
    depth = w_ada.shape[0]
    bsz = x.shape[0]
    mod = _ada(c, w_ada, b_ada)
    mod = mod.reshape(depth, bsz, N_MOD, 1, D_MODEL)
    for l in range(depth):
        sh1, sc1, g1, sh2, sc2, g2 = (mod[l, :, i] for i in range(N_MOD))
        x = _mixer(x, sh1, sc1, g1, mixer_norm_g[l], w_in[l].astype(_BF16), conv_w[l], conv_b[l],
                   conv_ln_g[l], conv_ln_b[l], w_conv_out[l].astype(_BF16), b_conv_out[l],
                   pool_w[l].astype(_BF16), pool_scale[l], w_out[l].astype(_BF16))
        w_rt, b_rt = _router_params(w_router_group[l], b_router_group[l],
                                    w_router_expert[l], b_router_expert[l])
        w_gu = jnp.concatenate([w_expert_gate[l], w_expert_up[l]], axis=-1).astype(_BF16)
        x = _moe(x, sh2, sc2, g2, ffn_norm_g[l], w_rt, b_rt, w_gu, w_expert_down[l].astype(_BF16),
                 final_norm_g, final_norm=(l == depth - 1))
    return x
```

```python
import functools

import jax
import jax.numpy as jnp
from jax import lax
from jax.experimental import pallas as pl
from jax.experimental.pallas import tpu as pltpu

D_MODEL = 1024
D_CONV = 512
D_POOL = 512
CONV_K = 31
POOL_WINDOWS = (2, 4, 8, 16)
POOL_GROUP = 128
POOL_OUT_GROUP = 256
N_GROUPS = 4
PER_GROUP = 8
N_EXPERTS = 32
TOP_K = 2
D_EXPERT = 256
N_MOD = 6
EPS = 1e-6

COL_POOL = 2 * D_CONV
COL_GATE_A = COL_POOL + D_POOL
COL_GATE_B = COL_GATE_A + D_MODEL
D_IN = COL_GATE_B + D_MODEL

LANES = 128
SUBLANES = 8
ROW_VREGS = D_MODEL // LANES
HALO = 32
SEQ_TILE = 512
CONV_ROWS = 128
CONV_ROW_STEPS = SEQ_TILE // CONV_ROWS
CONV_LANE_GROUPS = D_CONV // LANES
CONV_ACCS = 2
TAIL_ROWS = 128
GATE_COLS = 256
GATE_COL_BLOCKS = (D_IN - COL_GATE_A) // GATE_COLS
MOE_TILE = 4096
MOE_BLOCK = 512
MOE_BLOCKS = MOE_TILE // MOE_BLOCK
MOE_CHUNK = 256
MOE_SUB = 128
W_SLOTS = 4
CUM_BLOCK = 256
ROUTER_ROWS = 48
GROUP_ROW0 = N_EXPERTS
UNROLL = 8
VMEM_LIMIT = 56 * 1024 * 1024

_F32 = jnp.float32
_BF16 = jnp.bfloat16
assert ROW_VREGS == SUBLANES


def _sigmoid(v):
    return 1.0 / (1.0 + jnp.exp(-v))


def _silu(v):
    return v * _sigmoid(v)


def _rms_norm(v, g):
    return v * lax.rsqrt(jnp.mean(v * v, axis=-1, keepdims=True) + EPS) * g


def _ada_kernel(c_ref, w_ref, b_ref, o_ref):
    c_act = _silu(c_ref[...])
    o_ref[0] = jnp.dot(c_act, w_ref[0], preferred_element_type=_F32,
                       precision=lax.Precision.HIGHEST) + b_ref[0]


def _ada(c, w_ada, b_ada):
    depth, d, n = w_ada.shape
    tn = 1024
    return pl.pallas_call(
        _ada_kernel,
        grid=(depth, n // tn),
        in_specs=[
            pl.BlockSpec(c.shape, lambda l, j: (0, 0)),
            pl.BlockSpec((1, d, tn), lambda l, j: (l, 0, j)),
            pl.BlockSpec((1, 1, tn), lambda l, j: (l, 0, j)),
        ],
        out_specs=pl.BlockSpec((1, c.shape[0], tn), lambda l, j: (l, 0, j)),
        out_shape=jax.ShapeDtypeStruct((depth, c.shape[0], n), _F32),
        name="ada_mod",
    )(c, w_ada, b_ada.reshape(depth, 1, n))


def _mixer_kernel(x_ref, sh_ref, sc_ref, gt_ref, ng_ref, win_ref, wg_ref, cw_ref, cb_ref, lng_ref, lnb_ref,
                  wco_ref, bco_ref, pw_ref, ps_ref, wout_ref, o_ref,
                  gates_ref, e_ref, s_ref, conv_ref, pe_ref, p2_ref, p4_ref, p8_ref):
    ts = SEQ_TILE
    j = pl.program_id(1)

    @pl.when(j == 0)
    def _():
        e_ref[:, 0:HALO, :] = jnp.zeros((CONV_LANE_GROUPS, HALO, LANES), _F32)
        pe_ref[0:HALO, :] = jnp.zeros((HALO, D_POOL), _F32)

    @pl.when(j > 0)
    def _():
        e_ref[:, 0:HALO, :] = e_ref[:, ts:ts + HALO, :]
        pe_ref[0:HALO, :] = pe_ref[ts:ts + HALO, :]

    xt = x_ref[0]
    h = _rms_norm(xt, ng_ref[...]) * (1.0 + sc_ref[0]) + sh_ref[0]
    hb = h.astype(_BF16)

    a = jnp.dot(hb, win_ref[:, 0:D_CONV], preferred_element_type=_F32)
    g = jnp.dot(hb, win_ref[:, D_CONV:2 * D_CONV], preferred_element_type=_F32)
    glu = a * _sigmoid(g)
    for c in range(CONV_LANE_GROUPS):
        e_ref[c, HALO:HALO + ts, :] = glu[:, c * LANES:(c + 1) * LANES]
        for sft in range(1, SUBLANES):
            s_ref[sft - 1, c] = e_ref[c, sft:sft + ts + HALO - SUBLANES, :]

    def conv_block(c, r0, zero_row):
        accs = [jnp.broadcast_to(cb_ref[c] + zero_row, (CONV_ROWS, LANES))] + [None] * (CONV_ACCS - 1)
        for k in range(CONV_K):
            q, sft = divmod(HALO - (CONV_K - 1) + k, SUBLANES)
            lo = r0 + SUBLANES * q
            tap = e_ref[c, lo:lo + CONV_ROWS, :] if sft == 0 else s_ref[sft - 1, c, lo:lo + CONV_ROWS, :]
            prod = tap * cw_ref[k, c]
            a_i = k % CONV_ACCS
            accs[a_i] = prod if accs[a_i] is None else accs[a_i] + prod
        while len(accs) > 1:
            accs = [accs[p] + accs[p + 1] for p in range(0, len(accs), 2)]
        conv_ref[c, r0:r0 + CONV_ROWS, :] = accs[0]

    n_conv = CONV_LANE_GROUPS * CONV_ROW_STEPS
    convs_per_gate = n_conv // GATE_COL_BLOCKS
    for it in range(n_conv):
        if it % convs_per_gate == 0:
            n = it // convs_per_gate
            gate = _sigmoid(jnp.dot(hb, wg_ref[n], preferred_element_type=_F32))
            gates_ref[n] = gate
            zero_row = jnp.where(gate[ts - 1:ts, GATE_COLS - LANES:] > 2.0, 1.0, 0.0)
        conv_block(it // CONV_ROW_STEPS, (it % CONV_ROW_STEPS) * CONV_ROWS, zero_row)

    u = jnp.dot(hb, win_ref[:, COL_POOL:COL_GATE_A], preferred_element_type=_F32)
    pe_ref[HALO:HALO + ts, :] = u
    n_ext = ts + HALO
    p2_ref[8:n_ext, :] = pe_ref[8:n_ext, :] + pe_ref[7:n_ext - 1, :]
    p4_ref[16:n_ext, :] = p2_ref[16:n_ext, :] + p2_ref[14:n_ext - 2, :]
    p8_ref[24:n_ext, :] = p4_ref[24:n_ext, :] + p4_ref[20:n_ext - 4, :]

    half = GATE_COL_BLOCKS // 2
    for rb in range(ts // TAIL_ROWS):
        r0 = rb * TAIL_ROWS
        rows = slice(r0, r0 + TAIL_ROWS)
        ext = slice(HALO + r0, HALO + r0 + TAIL_ROWS)
        v = jnp.concatenate([conv_ref[c, rows, :] for c in range(CONV_LANE_GROUPS)], axis=1)
        mu = jnp.mean(v, axis=-1, keepdims=True)
        vc = v - mu
        var = jnp.mean(vc * vc, axis=-1, keepdims=True)
        v = _silu(vc * lax.rsqrt(var + EPS) * lng_ref[...] + lnb_ref[...])
        y_a = jnp.dot(v.astype(_BF16), wco_ref[...], preferred_element_type=_F32) + bco_ref[...]

        pos = j * ts + r0 + lax.broadcasted_iota(jnp.int32, (TAIL_ROWS, 1), 0) + 1
        y_b_parts = []
        for gi, w in enumerate(POOL_WINDOWS):
            cs = slice(gi * POOL_GROUP, (gi + 1) * POOL_GROUP)
            if w == 2:
                win_sum = p2_ref[ext, cs]
            elif w == 4:
                win_sum = p4_ref[ext, cs]
            elif w == 8:
                win_sum = p8_ref[ext, cs]
            else:
                win_sum = p8_ref[ext, cs] + p8_ref[HALO + r0 - 8:HALO + r0 - 8 + TAIL_ROWS, cs]
            count = jnp.minimum(pos, w).astype(_F32)
            pooled = win_sum / count - pe_ref[ext, cs]
            y_b_parts.append(jnp.dot(pooled.astype(_BF16), pw_ref[gi], preferred_element_type=_F32))
        y_b = jnp.concatenate(y_b_parts, axis=-1) * ps_ref[...]

        gate_a = jnp.concatenate([gates_ref[n, rows, :] for n in range(half)], axis=1)
        gate_b = jnp.concatenate([gates_ref[n, rows, :] for n in range(half, GATE_COL_BLOCKS)], axis=1)
        mixed = gate_a * y_a + gate_b * y_b
        out = jnp.dot(mixed.astype(_BF16), wout_ref[...], preferred_element_type=_F32)
        o_ref[0, rows, :] = x_ref[0, rows, :] + gt_ref[0] * out


def _mixer(x, sh, sc, gt, norm_g, w_in, conv_w, conv_b, ln_g, ln_b, w_co, b_co, pool_w, pool_scale, w_out):
    b, s, d = x.shape
    ts = SEQ_TILE
    const2 = lambda bi, j: (0, 0)
    const3 = lambda bi, j: (0, 0, 0)
    row = lambda a: a.reshape(1, -1)
    mod_spec = pl.BlockSpec((1, 1, d), lambda bi, j: (bi, 0, 0))
    w_main = w_in[:, :COL_GATE_A]
    w_gates = w_in[:, COL_GATE_A:].reshape(d, GATE_COL_BLOCKS, GATE_COLS).transpose(1, 0, 2)
    conv_w = conv_w.reshape(CONV_K, CONV_LANE_GROUPS, 1, LANES)
    conv_b = conv_b.reshape(CONV_LANE_GROUPS, 1, LANES)
    return pl.pallas_call(
        _mixer_kernel,
        grid=(b, s // ts),
        in_specs=[
            pl.BlockSpec((1, ts, d), lambda bi, j: (bi, j, 0)),
            mod_spec, mod_spec, mod_spec,
            pl.BlockSpec((1, d), const2),
            pl.BlockSpec(w_main.shape, const2),
            pl.BlockSpec(w_gates.shape, const3),
            pl.BlockSpec(conv_w.shape, lambda bi, j: (0, 0, 0, 0)),
            pl.BlockSpec(conv_b.shape, const3),
            pl.BlockSpec((1, D_CONV), const2),
            pl.BlockSpec((1, D_CONV), const2),
            pl.BlockSpec(w_co.shape, const2),
            pl.BlockSpec((1, d), const2),
            pl.BlockSpec(pool_w.shape, lambda bi, j: (0, 0, 0)),
            pl.BlockSpec((1, d), const2),
            pl.BlockSpec(w_out.shape, const2),
        ],
        out_specs=pl.BlockSpec((1, ts, d), lambda bi, j: (bi, j, 0)),
        out_shape=jax.ShapeDtypeStruct(x.shape, _F32),
        scratch_shapes=[
            pltpu.VMEM((GATE_COL_BLOCKS, ts, GATE_COLS), _F32),
            pltpu.VMEM((CONV_LANE_GROUPS, ts + HALO, LANES), _F32),
            pltpu.VMEM((SUBLANES - 1, CONV_LANE_GROUPS, ts + HALO - SUBLANES, LANES), _F32),
            pltpu.VMEM((CONV_LANE_GROUPS, ts, LANES), _F32),
            pltpu.VMEM((ts + HALO, D_POOL), _F32),
            pltpu.VMEM((ts + HALO, D_POOL), _F32),
            pltpu.VMEM((ts + HALO, D_POOL), _F32),
            pltpu.VMEM((ts + HALO, D_POOL), _F32),
        ],
        compiler_params=pltpu.CompilerParams(
            dimension_semantics=("arbitrary", "arbitrary"), vmem_limit_bytes=VMEM_LIMIT),
        name="mixer",
    )(x, sh, sc, gt, row(norm_g), w_main, w_gates, conv_w, conv_b, row(ln_g), row(ln_b), w_co, row(b_co),
      pool_w, row(pool_scale), w_out)


def _route_kernel(x_ref, sh_ref, sc_ref, ng_ref, wrt_ref, brt_ref, pos_ref, wgt_ref, seg_ref,
                  id_scr, rank_scr, wgt_scr, run_ref):
    tb = MOE_BLOCK
    j = pl.program_id(1)

    @pl.when(j == 0)
    def _():
        run_ref[...] = jnp.zeros_like(run_ref)

    h = _rms_norm(x_ref[...], ng_ref[...]) * (1.0 + sc_ref[0]) + sh_ref[0]
    nt = (((1,), (1,)), ((), ()))
    h_hi = h.astype(_BF16)
    h_lo = (h - h_hi.astype(_F32)).astype(_BF16)
    w = wrt_ref[...]
    w_hi = w.astype(_BF16)
    w_lo = (w - w_hi.astype(_F32)).astype(_BF16)
    part = lax.dot_general(jnp.concatenate([w_hi, w_lo], axis=0), h_hi, nt, preferred_element_type=_F32)
    logits = (part[0:ROUTER_ROWS] + part[ROUTER_ROWS:]
              + lax.dot_general(w_hi, h_lo, nt, preferred_element_type=_F32)
              + brt_ref[...])
    neg = jnp.float32(-jnp.inf)
    big = jnp.int32(N_EXPERTS)

    grow = lax.broadcasted_iota(jnp.int32, (SUBLANES, tb), 0)
    glog = jnp.where(grow < N_GROUPS, logits[GROUP_ROW0:GROUP_ROW0 + SUBLANES, :], neg)
    gmax = jnp.max(glog, axis=0, keepdims=True)
    p_group = 1.0 / jnp.sum(jnp.exp(glog - gmax), axis=0, keepdims=True)
    g_idx = jnp.min(jnp.where(glog == gmax, grow, big), axis=0, keepdims=True)

    erow = lax.broadcasted_iota(jnp.int32, (N_EXPERTS, tb), 0)
    el = jnp.where((erow // PER_GROUP) == g_idx, logits[0:N_EXPERTS, :], neg)
    m1 = jnp.max(el, axis=0, keepdims=True)
    i1 = jnp.min(jnp.where(el == m1, erow, big), axis=0, keepdims=True)
    el2 = jnp.where(erow == i1, neg, el)
    m2 = jnp.max(el2, axis=0, keepdims=True)
    i2 = jnp.min(jnp.where(el2 == m2, erow, big), axis=0, keepdims=True)
    r = jnp.exp(m2 - m1)
    w1 = p_group / (1.0 + r)
    w2 = w1 * r
    sel1 = erow == i1
    sel2 = erow == i2

    used = jnp.where(sel1 | sel2, 1.0, 0.0)
    tri_r = lax.broadcasted_iota(jnp.int32, (CUM_BLOCK, CUM_BLOCK), 0)
    tri_c = lax.broadcasted_iota(jnp.int32, (CUM_BLOCK, CUM_BLOCK), 1)
    upper = jnp.where(tri_r < tri_c, 1.0, 0.0).astype(_BF16)
    run = run_ref[:, 0:1]
    ranks = []
    for blk in range(tb // CUM_BLOCK):
        ub = used[:, blk * CUM_BLOCK:(blk + 1) * CUM_BLOCK]
        ranks.append(jnp.dot(ub.astype(_BF16), upper, preferred_element_type=_F32) + run)
        run = run + jnp.sum(ub, axis=1, keepdims=True)
    rank = jnp.concatenate(ranks, axis=1)
    run_ref[...] = jnp.broadcast_to(run, run_ref.shape)
    id_scr[j, 0:1, :] = i1
    id_scr[j, 1:2, :] = i2
    rank_scr[j, 0:1, :] = jnp.sum(jnp.where(sel1, rank, 0.0), axis=0, keepdims=True)
    rank_scr[j, 1:2, :] = jnp.sum(jnp.where(sel2, rank, 0.0), axis=0, keepdims=True)
    wgt_scr[j, 0:1, :] = w1
    wgt_scr[j, 1:2, :] = w2

    @pl.when(j == MOE_BLOCKS - 1)
    def _():
        cnt = jnp.concatenate([run_ref[...], jnp.zeros((LANES - N_EXPERTS, LANES), _F32)], axis=0)
        low_r = lax.broadcasted_iota(jnp.int32, (LANES, LANES), 0)
        low_c = lax.broadcasted_iota(jnp.int32, (LANES, LANES), 1)
        lower = jnp.where(low_c < low_r, 1.0, 0.0)
        off = jnp.dot(lower, cnt, preferred_element_type=_F32, precision=lax.Precision.HIGHEST)
        seg_ref[0, 0:N_EXPERTS, :] = off[0:N_EXPERTS, :].astype(jnp.int32)
        seg_ref[0, N_EXPERTS:2 * N_EXPERTS, :] = cnt[0:N_EXPERTS, :].astype(jnp.int32)
        off_col = off[0:N_EXPERTS, 0:1]
        for blk in range(MOE_BLOCKS):
            ls = slice(blk * tb, (blk + 1) * tb)
            for k in range(TOP_K):
                start = jnp.sum(jnp.where(erow == id_scr[blk, k:k + 1, :], off_col, 0.0),
                                axis=0, keepdims=True)
                slot = start + rank_scr[blk, k:k + 1, :]
                pos_ref[0, k:k + 1, ls] = (slot * ROW_VREGS).astype(jnp.int32)
                wgt_ref[0, k:k + 1, ls] = wgt_scr[blk, k:k + 1, :]


def _route(xf, sh, sc, norm_g, w_rt, b_rt, per_b):
    t, d = xf.shape
    tm, tb, nb = MOE_TILE, MOE_BLOCK, MOE_BLOCKS
    n_tiles = t // tm
    const2 = lambda i, j: (0, 0)
    mod_spec = pl.BlockSpec((1, 1, d), lambda i, j: (i // per_b, 0, 0))
    out3 = lambda i, j: (i, 0, 0)
    return pl.pallas_call(
        _route_kernel,
        grid=(n_tiles, nb),
        in_specs=[
            pl.BlockSpec((tb, d), lambda i, j: (i * nb + j, 0)),
            mod_spec, mod_spec,
            pl.BlockSpec((1, d), const2),
            pl.BlockSpec((ROUTER_ROWS, d), const2),
            pl.BlockSpec((ROUTER_ROWS, 1), const2),
        ],
        out_specs=[
            pl.BlockSpec((1, TOP_K, tm), out3),
            pl.BlockSpec((1, TOP_K, tm), out3),
            pl.BlockSpec((1, 2 * N_EXPERTS, LANES), out3),
        ],
        out_shape=[
            jax.ShapeDtypeStruct((n_tiles, TOP_K, tm), jnp.int32),
            jax.ShapeDtypeStruct((n_tiles, TOP_K, tm), _F32),
            jax.ShapeDtypeStruct((n_tiles, 2 * N_EXPERTS, LANES), jnp.int32),
        ],
        scratch_shapes=[
            pltpu.VMEM((nb, TOP_K, tb), jnp.int32),
            pltpu.VMEM((nb, TOP_K, tb), _F32),
            pltpu.VMEM((nb, TOP_K, tb), _F32),
            pltpu.VMEM((N_EXPERTS, LANES), _F32),
        ],
        compiler_params=pltpu.CompilerParams(
            dimension_semantics=("arbitrary", "arbitrary"), vmem_limit_bytes=VMEM_LIMIT),
        name="route",
    )(xf, sh, sc, norm_g.reshape(1, d), w_rt, b_rt)


def _row_ds(offset):
    return pl.ds(pl.multiple_of(offset, ROW_VREGS), ROW_VREGS)


def _moe_kernel(pos_s, wgt_s, seg_s,
                x_ref, sh_ref, sc_ref, gt_ref, ng_ref, fg_ref, wg_hbm, wu_hbm, wd_hbm, o_ref,
                tm_ref, srt_ref, wgu_buf, wd_buf, sem, *, final_norm):
    tm, tb, nb, ch = MOE_TILE, MOE_BLOCK, MOE_BLOCKS, MOE_CHUNK
    i = pl.program_id(0)
    j = pl.program_id(1)
    base = i * (TOP_K * tm) + (j % nb) * tb

    def weight_copies(e, slot):
        return (pltpu.make_async_copy(wg_hbm.at[e], wgu_buf.at[slot, :, 0:D_EXPERT], sem.at[0, slot]),
                pltpu.make_async_copy(wu_hbm.at[e], wgu_buf.at[slot, :, D_EXPERT:], sem.at[1, slot]),
                pltpu.make_async_copy(wd_hbm.at[e], wd_buf.at[slot], sem.at[2, slot]))

    @pl.when(j == 0)
    def _():
        for e0 in range(W_SLOTS - 1):
            for cp in weight_copies(e0, e0):
                cp.start()
        srt_ref[TOP_K * tm * ROW_VREGS:, :] = jnp.zeros((ch * ROW_VREGS, LANES), _F32)

    @pl.when(j < nb)
    def _():
        h = _rms_norm(x_ref[...], ng_ref[...]) * (1.0 + sc_ref[0]) + sh_ref[0]
        for s in range(ROW_VREGS):
            tm_ref[pl.ds(s, tb, stride=ROW_VREGS), :] = h[:, s * LANES:(s + 1) * LANES]

        def dispatch(it, carry):
            t0 = it * UNROLL
            rows = [tm_ref[_row_ds((t0 + u) * ROW_VREGS), :] for u in range(UNROLL)]
            for u in range(UNROLL):
                for k in range(TOP_K):
                    srt_ref[_row_ds(pos_s[base + k * tm + t0 + u]), :] = rows[u]
            return carry

        lax.fori_loop(0, tb // UNROLL, dispatch, 0)

    @pl.when(j == nb)
    def _():
        def expert(e, carry):
            slot = e % W_SLOTS
            for cp in weight_copies(e, slot):
                cp.wait()

            @pl.when(e + W_SLOTS - 1 < N_EXPERTS)
            def _():
                for cp in weight_copies(e + W_SLOTS - 1, (e + W_SLOTS - 1) % W_SLOTS):
                    cp.start()

            off = seg_s[i * (2 * N_EXPERTS) + e]
            cnt = seg_s[i * (2 * N_EXPERTS) + N_EXPERTS + e]

            def run_rows(first_row, n_sub):
                subs = range(n_sub)
                row0 = [first_row + sub * MOE_SUB for sub in subs]
                r0 = [(off + r) * ROW_VREGS for r in row0]
                xs = [jnp.concatenate([srt_ref[pl.ds(r0[sub] + s, MOE_SUB, stride=ROW_VREGS), :]
                                       for s in range(ROW_VREGS)], axis=1) for sub in subs]
                ys = []
                hgus = [jnp.dot(xs[sub].astype(_BF16), wgu_buf[slot], preferred_element_type=_F32)
                        for sub in subs]
                for sub in subs:
                    hgu = hgus[sub]
                    act = _silu(hgu[:, :D_EXPERT]) * hgu[:, D_EXPERT:]
                    y = jnp.dot(act.astype(_BF16), wd_buf[slot], preferred_element_type=_F32)
                    valid = lax.broadcasted_iota(jnp.int32, (MOE_SUB, 1), 0) < (cnt - row0[sub])
                    ys.append(jnp.where(valid, y, xs[sub]))
                for sub in subs:
                    for s in range(ROW_VREGS):
                        srt_ref[pl.ds(r0[sub] + s, MOE_SUB, stride=ROW_VREGS), :] = (
                            ys[sub][:, s * LANES:(s + 1) * LANES])

            def chunk(c, inner):
                run_rows(c * ch, ch // MOE_SUB)
                return inner

            n_whole = cnt // ch
            lax.fori_loop(0, n_whole, chunk, 0)
            rest = cnt - n_whole * ch
            for n_sub in range(1, ch // MOE_SUB + 1):
                @pl.when((rest > (n_sub - 1) * MOE_SUB) & (rest <= n_sub * MOE_SUB))
                def _(n_sub=n_sub):
                    run_rows(n_whole * ch, n_sub)
            return carry

        lax.fori_loop(0, N_EXPERTS, expert, 0)

    @pl.when(j >= nb)
    def _():
        def combine(it, carry):
            t0 = it * UNROLL
            outs = []
            for u in range(UNROLL):
                t = base + t0 + u
                y0 = srt_ref[_row_ds(pos_s[t]), :]
                y1 = srt_ref[_row_ds(pos_s[t + tm]), :]
                outs.append(wgt_s[t] * y0 + wgt_s[t + tm] * y1)
            for u in range(UNROLL):
                tm_ref[_row_ds((t0 + u) * ROW_VREGS), :] = outs[u]
            return carry

        lax.fori_loop(0, tb // UNROLL, combine, 0)

        moe = jnp.concatenate(
            [tm_ref[pl.ds(s, tb, stride=ROW_VREGS), :] for s in range(ROW_VREGS)], axis=1)
        y = x_ref[...] + gt_ref[0] * moe
        if final_norm:
            y = _rms_norm(y, fg_ref[...])
        o_ref[...] = y


def _moe(x, sh, sc, gt, norm_g, w_rt, b_rt, w_g, w_u, w_d, final_g, final_norm):
    b, s, d = x.shape
    t = b * s
    tm, tb, nb = MOE_TILE, MOE_BLOCK, MOE_BLOCKS
    per_b = s // tm
    n_tiles = t // tm
    xf = x.reshape(t, d)
    pos, wgt, seg = _route(xf, sh, sc, norm_g, w_rt, b_rt, per_b)
    const2 = lambda i, j, *_: (0, 0)
    mod_spec = pl.BlockSpec((1, 1, d), lambda i, j, *_: (i // per_b, 0, 0))
    out = pl.pallas_call(
        functools.partial(_moe_kernel, final_norm=final_norm),
        grid_spec=pltpu.PrefetchScalarGridSpec(
            num_scalar_prefetch=3,
            grid=(n_tiles, 2 * nb),
            in_specs=[
                pl.BlockSpec((tb, d), lambda i, j, *_: (i * nb + j % nb, 0)),
                mod_spec, mod_spec, mod_spec,
                pl.BlockSpec((1, d), const2),
                pl.BlockSpec((1, d), const2),
                pl.BlockSpec(memory_space=pl.ANY),
                pl.BlockSpec(memory_space=pl.ANY),
                pl.BlockSpec(memory_space=pl.ANY),
            ],
            out_specs=pl.BlockSpec((tb, d), lambda i, j, *_: (i * nb + jnp.maximum(j - nb, 0), 0)),
            scratch_shapes=[
                pltpu.VMEM((tb * ROW_VREGS, LANES), _F32),
                pltpu.VMEM(((TOP_K * tm + MOE_CHUNK) * ROW_VREGS, LANES), _F32),
                pltpu.VMEM((W_SLOTS, d, 2 * D_EXPERT), _BF16),
                pltpu.VMEM((W_SLOTS, D_EXPERT, d), _BF16),
                pltpu.SemaphoreType.DMA((3, W_SLOTS)),
            ],
        ),
        out_shape=jax.ShapeDtypeStruct((t, d), _F32),
        compiler_params=pltpu.CompilerParams(
            dimension_semantics=("arbitrary", "arbitrary"), vmem_limit_bytes=VMEM_LIMIT),
        name="moe",
    )(pos.reshape(-1), wgt.reshape(-1), seg[:, :, 0].reshape(-1),
      xf, sh, sc, gt, norm_g.reshape(1, d), final_g.reshape(1, d), w_g, w_u, w_d)
    return out.reshape(b, s, d)


def _router_params(w_rg, b_rg, w_re, b_re):
    d = w_rg.shape[0]
    pad = ROUTER_ROWS - N_EXPERTS - N_GROUPS
    w = jnp.concatenate([w_re.T, w_rg.T, jnp.zeros((pad, d), _F32)], axis=0)
    b = jnp.concatenate([b_re, b_rg, jnp.zeros((pad,), _F32)]).reshape(ROUTER_ROWS, 1)
    return w, b


def kernel(x, c, mixer_norm_g, w_ada, b_ada, w_in, conv_w, conv_b, conv_ln_g, conv_ln_b, w_conv_out,
           b_conv_out, pool_w, pool_scale, w_out, ffn_norm_g, w_router_group, b_router_group,
           w_router_expert, b_router_expert, w_expert_gate, w_expert_up, w_expert_down, final_norm_g):
    depth = w_ada.shape[0]
    bsz = x.shape[0]
    mod = _ada(c, w_ada, b_ada)
    mod = mod.reshape(depth, bsz, N_MOD, 1, D_MODEL)
    for l in range(depth):
        sh1, sc1, g1, sh2, sc2, g2 = (mod[l, :, i] for i in range(N_MOD))
        x = _mixer(x, sh1, sc1, g1, mixer_norm_g[l], w_in[l].astype(_BF16), conv_w[l], conv_b[l],
                   conv_ln_g[l], conv_ln_b[l], w_conv_out[l].astype(_BF16), b_conv_out[l],
                   pool_w[l].astype(_BF16), pool_scale[l], w_out[l].astype(_BF16))
        w_rt, b_rt = _router_params(w_router_group[l], b_router_group[l],
                                    w_router_expert[l], b_router_expert[l])
        x = _moe(x, sh2, sc2, g2, ffn_norm_g[l], w_rt, b_rt, w_expert_gate[l].astype(_BF16),
                 w_expert_up[l].astype(_BF16), w_expert_down[l].astype(_BF16),
                 final_norm_g, final_norm=(l == depth - 1))
    return x
```

```python
import functools

import jax
import jax.numpy as jnp
from jax import lax
from jax.experimental import pallas as pl
from jax.experimental.pallas import tpu as pltpu

D_MODEL = 1024
D_CONV = 512
D_POOL = 512
CONV_K = 31
POOL_WINDOWS = (2, 4, 8, 16)
POOL_GROUP = 128
POOL_OUT_GROUP = 256
N_GROUPS = 4
PER_GROUP = 8
N_EXPERTS = 32
TOP_K = 2
D_EXPERT = 256
N_MOD = 6
EPS = 1e-6

COL_POOL = 2 * D_CONV
COL_GATE_A = COL_POOL + D_POOL
COL_GATE_B = COL_GATE_A + D_MODEL
D_IN = COL_GATE_B + D_MODEL

LANES = 128
SUBLANES = 8
ROW_VREGS = D_MODEL // LANES
HALO = 32
SEQ_TILE = 512
CONV_ROWS = 128
CONV_ROW_STEPS = SEQ_TILE // CONV_ROWS
CONV_LANE_GROUPS = D_CONV // LANES
CONV_ACCS = 2
TAIL_ROWS = 128
GATE_COLS = 256
GATE_COL_BLOCKS = (D_IN - COL_GATE_A) // GATE_COLS
MOE_TILE = 4096
MOE_BLOCK = 512
MOE_BLOCKS = MOE_TILE // MOE_BLOCK
MOE_CHUNK = 256
MOE_SUB = 128
W_SLOTS = 4
CUM_BLOCK = 256
ROUTER_ROWS = 48
GROUP_ROW0 = N_EXPERTS
UNROLL = 8
VMEM_LIMIT = 56 * 1024 * 1024

_F32 = jnp.float32
_BF16 = jnp.bfloat16
assert ROW_VREGS == SUBLANES


def _sigmoid(v):
    return 1.0 / (1.0 + jnp.exp(-v))


def _silu(v):
    return v * _sigmoid(v)


def _rms_norm(v, g):
    return v * lax.rsqrt(jnp.mean(v * v, axis=-1, keepdims=True) + EPS) * g


def _ada_kernel(c_ref, w_ref, b_ref, o_ref):
    c_act = _silu(c_ref[...])
    o_ref[0] = jnp.dot(c_act, w_ref[0], preferred_element_type=_F32,
                       precision=lax.Precision.HIGHEST) + b_ref[0]


def _ada(c, w_ada, b_ada):
    depth, d, n = w_ada.shape
    tn = 1024
    return pl.pallas_call(
        _ada_kernel,
        grid=(depth, n // tn),
        in_specs=[
            pl.BlockSpec(c.shape, lambda l, j: (0, 0)),
            pl.BlockSpec((1, d, tn), lambda l, j: (l, 0, j)),
            pl.BlockSpec((1, 1, tn), lambda l, j: (l, 0, j)),
        ],
        out_specs=pl.BlockSpec((1, c.shape[0], tn), lambda l, j: (l, 0, j)),
        out_shape=jax.ShapeDtypeStruct((depth, c.shape[0], n), _F32),
        name="ada_mod",
    )(c, w_ada, b_ada.reshape(depth, 1, n))


def _mixer_kernel(x_ref, sh_ref, sc_ref, gt_ref, ng_ref, win_ref, cw_ref, cb_ref, lng_ref, lnb_ref,
                  wco_ref, bco_ref, pw_ref, ps_ref, wout_ref, o_ref,
                  gates_ref, e_ref, s_ref, conv_ref, pe_ref, p2_ref, p4_ref, p8_ref):
    ts = SEQ_TILE
    j = pl.program_id(1)

    @pl.when(j == 0)
    def _():
        e_ref[:, 0:HALO, :] = jnp.zeros((CONV_LANE_GROUPS, HALO, LANES), _F32)
        pe_ref[0:HALO, :] = jnp.zeros((HALO, D_POOL), _F32)

    @pl.when(j > 0)
    def _():
        e_ref[:, 0:HALO, :] = e_ref[:, ts:ts + HALO, :]
        pe_ref[0:HALO, :] = pe_ref[ts:ts + HALO, :]

    xt = x_ref[0]
    h = _rms_norm(xt, ng_ref[...]) * (1.0 + sc_ref[0]) + sh_ref[0]
    hb = h.astype(_BF16)

    a = jnp.dot(hb, win_ref[:, 0:D_CONV], preferred_element_type=_F32)
    g = jnp.dot(hb, win_ref[:, D_CONV:2 * D_CONV], preferred_element_type=_F32)
    glu = a * _sigmoid(g)
    for c in range(CONV_LANE_GROUPS):
        e_ref[c, HALO:HALO + ts, :] = glu[:, c * LANES:(c + 1) * LANES]
        for sft in range(1, SUBLANES):
            s_ref[sft - 1, c] = e_ref[c, sft:sft + ts + HALO - SUBLANES, :]

    def conv_block(c, r0, zero_row):
        accs = [jnp.broadcast_to(cb_ref[c] + zero_row, (CONV_ROWS, LANES))] + [None] * (CONV_ACCS - 1)
        for k in range(CONV_K):
            q, sft = divmod(HALO - (CONV_K - 1) + k, SUBLANES)
            lo = r0 + SUBLANES * q
            tap = e_ref[c, lo:lo + CONV_ROWS, :] if sft == 0 else s_ref[sft - 1, c, lo:lo + CONV_ROWS, :]
            prod = tap * cw_ref[k, c]
            a_i = k % CONV_ACCS
            accs[a_i] = prod if accs[a_i] is None else accs[a_i] + prod
        while len(accs) > 1:
            accs = [accs[p] + accs[p + 1] for p in range(0, len(accs), 2)]
        conv_ref[c, r0:r0 + CONV_ROWS, :] = accs[0]

    n_conv = CONV_LANE_GROUPS * CONV_ROW_STEPS
    convs_per_gate = n_conv // GATE_COL_BLOCKS
    for it in range(n_conv):
        if it % convs_per_gate == 0:
            n = it // convs_per_gate
            col = COL_GATE_A + n * GATE_COLS
            gate = _sigmoid(jnp.dot(hb, win_ref[:, col:col + GATE_COLS], preferred_element_type=_F32))
            gates_ref[n] = gate
            zero_row = jnp.where(gate[ts - 1:ts, GATE_COLS - LANES:] > 2.0, 1.0, 0.0)
        conv_block(it // CONV_ROW_STEPS, (it % CONV_ROW_STEPS) * CONV_ROWS, zero_row)

    u = jnp.dot(hb, win_ref[:, COL_POOL:COL_GATE_A], preferred_element_type=_F32)
    pe_ref[HALO:HALO + ts, :] = u
    n_ext = ts + HALO
    p2_ref[8:n_ext, :] = pe_ref[8:n_ext, :] + pe_ref[7:n_ext - 1, :]
    p4_ref[16:n_ext, :] = p2_ref[16:n_ext, :] + p2_ref[14:n_ext - 2, :]
    p8_ref[24:n_ext, :] = p4_ref[24:n_ext, :] + p4_ref[20:n_ext - 4, :]

    half = GATE_COL_BLOCKS // 2
    for rb in range(ts // TAIL_ROWS):
        r0 = rb * TAIL_ROWS
        rows = slice(r0, r0 + TAIL_ROWS)
        ext = slice(HALO + r0, HALO + r0 + TAIL_ROWS)
        v = jnp.concatenate([conv_ref[c, rows, :] for c in range(CONV_LANE_GROUPS)], axis=1)
        mu = jnp.mean(v, axis=-1, keepdims=True)
        vc = v - mu
        var = jnp.mean(vc * vc, axis=-1, keepdims=True)
        v = _silu(vc * lax.rsqrt(var + EPS) * lng_ref[...] + lnb_ref[...])
        y_a = jnp.dot(v.astype(_BF16), wco_ref[...], preferred_element_type=_F32) + bco_ref[...]

        pos = j * ts + r0 + lax.broadcasted_iota(jnp.int32, (TAIL_ROWS, 1), 0) + 1
        y_b_parts = []
        for gi, w in enumerate(POOL_WINDOWS):
            cs = slice(gi * POOL_GROUP, (gi + 1) * POOL_GROUP)
            if w == 2:
                win_sum = p2_ref[ext, cs]
            elif w == 4:
                win_sum = p4_ref[ext, cs]
            elif w == 8:
                win_sum = p8_ref[ext, cs]
            else:
                win_sum = p8_ref[ext, cs] + p8_ref[HALO + r0 - 8:HALO + r0 - 8 + TAIL_ROWS, cs]
            count = jnp.minimum(pos, w).astype(_F32)
            pooled = win_sum / count - pe_ref[ext, cs]
            y_b_parts.append(jnp.dot(pooled.astype(_BF16), pw_ref[gi], preferred_element_type=_F32))
        y_b = jnp.concatenate(y_b_parts, axis=-1) * ps_ref[...]

        gate_a = jnp.concatenate([gates_ref[n, rows, :] for n in range(half)], axis=1)
        gate_b = jnp.concatenate([gates_ref[n, rows, :] for n in range(half, GATE_COL_BLOCKS)], axis=1)
        mixed = gate_a * y_a + gate_b * y_b
        out = jnp.dot(mixed.astype(_BF16), wout_ref[...], preferred_element_type=_F32)
        o_ref[0, rows, :] = x_ref[0, rows, :] + gt_ref[0] * out


def _mixer(x, sh, sc, gt, norm_g, w_in, conv_w, conv_b, ln_g, ln_b, w_co, b_co, pool_w, pool_scale, w_out):
    b, s, d = x.shape
    ts = SEQ_TILE
    const2 = lambda bi, j: (0, 0)
    const3 = lambda bi, j: (0, 0, 0)
    row = lambda a: a.reshape(1, -1)
    mod_spec = pl.BlockSpec((1, 1, d), lambda bi, j: (bi, 0, 0))
    conv_w = conv_w.reshape(CONV_K, CONV_LANE_GROUPS, 1, LANES)
    conv_b = conv_b.reshape(CONV_LANE_GROUPS, 1, LANES)
    return pl.pallas_call(
        _mixer_kernel,
        grid=(b, s // ts),
        in_specs=[
            pl.BlockSpec((1, ts, d), lambda bi, j: (bi, j, 0)),
            mod_spec, mod_spec, mod_spec,
            pl.BlockSpec((1, d), const2),
            pl.BlockSpec(w_in.shape, const2),
            pl.BlockSpec(conv_w.shape, lambda bi, j: (0, 0, 0, 0)),
            pl.BlockSpec(conv_b.shape, const3),
            pl.BlockSpec((1, D_CONV), const2),
            pl.BlockSpec((1, D_CONV), const2),
            pl.BlockSpec(w_co.shape, const2),
            pl.BlockSpec((1, d), const2),
            pl.BlockSpec(pool_w.shape, lambda bi, j: (0, 0, 0)),
            pl.BlockSpec((1, d), const2),
            pl.BlockSpec(w_out.shape, const2),
        ],
        out_specs=pl.BlockSpec((1, ts, d), lambda bi, j: (bi, j, 0)),
        out_shape=jax.ShapeDtypeStruct(x.shape, _F32),
        scratch_shapes=[
            pltpu.VMEM((GATE_COL_BLOCKS, ts, GATE_COLS), _F32),
            pltpu.VMEM((CONV_LANE_GROUPS, ts + HALO, LANES), _F32),
            pltpu.VMEM((SUBLANES - 1, CONV_LANE_GROUPS, ts + HALO - SUBLANES, LANES), _F32),
            pltpu.VMEM((CONV_LANE_GROUPS, ts, LANES), _F32),
            pltpu.VMEM((ts + HALO, D_POOL), _F32),
            pltpu.VMEM((ts + HALO, D_POOL), _F32),
            pltpu.VMEM((ts + HALO, D_POOL), _F32),
            pltpu.VMEM((ts + HALO, D_POOL), _F32),
        ],
        compiler_params=pltpu.CompilerParams(
            dimension_semantics=("arbitrary", "arbitrary"), vmem_limit_bytes=VMEM_LIMIT),
        name="mixer",
    )(x, sh, sc, gt, row(norm_g), w_in, conv_w, conv_b, row(ln_g), row(ln_b), w_co, row(b_co),
      pool_w, row(pool_scale), w_out)


def _route_kernel(x_ref, sh_ref, sc_ref, ng_ref, wrt_ref, brt_ref, htm_ref, pos_ref, wgt_ref, seg_ref,
                  id_scr, rank_scr, wgt_scr, run_ref):
    tb = MOE_BLOCK
    j = pl.program_id(1)

    @pl.when(j == 0)
    def _():
        run_ref[...] = jnp.zeros_like(run_ref)

    h = _rms_norm(x_ref[...], ng_ref[...]) * (1.0 + sc_ref[0]) + sh_ref[0]
    for s in range(ROW_VREGS):
        htm_ref[pl.ds(s, tb, stride=ROW_VREGS), :] = h[:, s * LANES:(s + 1) * LANES]
    nt = (((1,), (1,)), ((), ()))
    h_hi = h.astype(_BF16)
    h_lo = (h - h_hi.astype(_F32)).astype(_BF16)
    w = wrt_ref[...]
    w_hi = w.astype(_BF16)
    w_lo = (w - w_hi.astype(_F32)).astype(_BF16)
    part = lax.dot_general(jnp.concatenate([w_hi, w_lo], axis=0), h_hi, nt, preferred_element_type=_F32)
    logits = (part[0:ROUTER_ROWS] + part[ROUTER_ROWS:]
              + lax.dot_general(w_hi, h_lo, nt, preferred_element_type=_F32)
              + brt_ref[...])
    neg = jnp.float32(-jnp.inf)
    big = jnp.int32(N_EXPERTS)

    grow = lax.broadcasted_iota(jnp.int32, (SUBLANES, tb), 0)
    glog = jnp.where(grow < N_GROUPS, logits[GROUP_ROW0:GROUP_ROW0 + SUBLANES, :], neg)
    gmax = jnp.max(glog, axis=0, keepdims=True)
    p_group = 1.0 / jnp.sum(jnp.exp(glog - gmax), axis=0, keepdims=True)
    g_idx = jnp.min(jnp.where(glog == gmax, grow, big), axis=0, keepdims=True)

    erow = lax.broadcasted_iota(jnp.int32, (N_EXPERTS, tb), 0)
    el = jnp.where((erow // PER_GROUP) == g_idx, logits[0:N_EXPERTS, :], neg)
    m1 = jnp.max(el, axis=0, keepdims=True)
    i1 = jnp.min(jnp.where(el == m1, erow, big), axis=0, keepdims=True)
    el2 = jnp.where(erow == i1, neg, el)
    m2 = jnp.max(el2, axis=0, keepdims=True)
    i2 = jnp.min(jnp.where(el2 == m2, erow, big), axis=0, keepdims=True)
    r = jnp.exp(m2 - m1)
    w1 = p_group / (1.0 + r)
    w2 = w1 * r
    sel1 = erow == i1
    sel2 = erow == i2

    used = jnp.where(sel1 | sel2, 1.0, 0.0)
    tri_r = lax.broadcasted_iota(jnp.int32, (CUM_BLOCK, CUM_BLOCK), 0)
    tri_c = lax.broadcasted_iota(jnp.int32, (CUM_BLOCK, CUM_BLOCK), 1)
    upper = jnp.where(tri_r < tri_c, 1.0, 0.0).astype(_BF16)
    run = run_ref[:, 0:1]
    ranks = []
    for blk in range(tb // CUM_BLOCK):
        ub = used[:, blk * CUM_BLOCK:(blk + 1) * CUM_BLOCK]
        ranks.append(jnp.dot(ub.astype(_BF16), upper, preferred_element_type=_F32) + run)
        run = run + jnp.sum(ub, axis=1, keepdims=True)
    rank = jnp.concatenate(ranks, axis=1)
    run_ref[...] = jnp.broadcast_to(run, run_ref.shape)
    id_scr[j, 0:1, :] = i1
    id_scr[j, 1:2, :] = i2
    rank_scr[j, 0:1, :] = jnp.sum(jnp.where(sel1, rank, 0.0), axis=0, keepdims=True)
    rank_scr[j, 1:2, :] = jnp.sum(jnp.where(sel2, rank, 0.0), axis=0, keepdims=True)
    wgt_scr[j, 0:1, :] = w1
    wgt_scr[j, 1:2, :] = w2

    @pl.when(j == MOE_BLOCKS - 1)
    def _():
        cnt = jnp.concatenate([run_ref[...], jnp.zeros((LANES - N_EXPERTS, LANES), _F32)], axis=0)
        low_r = lax.broadcasted_iota(jnp.int32, (LANES, LANES), 0)
        low_c = lax.broadcasted_iota(jnp.int32, (LANES, LANES), 1)
        lower = jnp.where(low_c < low_r, 1.0, 0.0)
        off = jnp.dot(lower, cnt, preferred_element_type=_F32, precision=lax.Precision.HIGHEST)
        seg_ref[0, 0:N_EXPERTS, :] = off[0:N_EXPERTS, :].astype(jnp.int32)
        seg_ref[0, N_EXPERTS:2 * N_EXPERTS, :] = cnt[0:N_EXPERTS, :].astype(jnp.int32)
        off_col = off[0:N_EXPERTS, 0:1]
        for blk in range(MOE_BLOCKS):
            ls = slice(blk * tb, (blk + 1) * tb)
            for k in range(TOP_K):
                start = jnp.sum(jnp.where(erow == id_scr[blk, k:k + 1, :], off_col, 0.0),
                                axis=0, keepdims=True)
                slot = start + rank_scr[blk, k:k + 1, :]
                pos_ref[0, k:k + 1, ls] = (slot * ROW_VREGS).astype(jnp.int32)
                wgt_ref[0, k:k + 1, ls] = wgt_scr[blk, k:k + 1, :]


def _route(xf, sh, sc, norm_g, w_rt, b_rt, per_b):
    t, d = xf.shape
    tm, tb, nb = MOE_TILE, MOE_BLOCK, MOE_BLOCKS
    n_tiles = t // tm
    const2 = lambda i, j: (0, 0)
    mod_spec = pl.BlockSpec((1, 1, d), lambda i, j: (i // per_b, 0, 0))
    out3 = lambda i, j: (i, 0, 0)
    return pl.pallas_call(
        _route_kernel,
        grid=(n_tiles, nb),
        in_specs=[
            pl.BlockSpec((tb, d), lambda i, j: (i * nb + j, 0)),
            mod_spec, mod_spec,
            pl.BlockSpec((1, d), const2),
            pl.BlockSpec((ROUTER_ROWS, d), const2),
            pl.BlockSpec((ROUTER_ROWS, 1), const2),
        ],
        out_specs=[
            pl.BlockSpec((tb * ROW_VREGS, LANES), lambda i, j: (i * nb + j, 0)),
            pl.BlockSpec((1, TOP_K, tm), out3),
            pl.BlockSpec((1, TOP_K, tm), out3),
            pl.BlockSpec((1, 2 * N_EXPERTS, LANES), out3),
        ],
        out_shape=[
            jax.ShapeDtypeStruct((t * ROW_VREGS, LANES), _F32),
            jax.ShapeDtypeStruct((n_tiles, TOP_K, tm), jnp.int32),
            jax.ShapeDtypeStruct((n_tiles, TOP_K, tm), _F32),
            jax.ShapeDtypeStruct((n_tiles, 2 * N_EXPERTS, LANES), jnp.int32),
        ],
        scratch_shapes=[
            pltpu.VMEM((nb, TOP_K, tb), jnp.int32),
            pltpu.VMEM((nb, TOP_K, tb), _F32),
            pltpu.VMEM((nb, TOP_K, tb), _F32),
            pltpu.VMEM((N_EXPERTS, LANES), _F32),
        ],
        compiler_params=pltpu.CompilerParams(
            dimension_semantics=("arbitrary", "arbitrary"), vmem_limit_bytes=VMEM_LIMIT),
        name="route",
    )(xf, sh, sc, norm_g.reshape(1, d), w_rt, b_rt)


def _row_ds(offset):
    return pl.ds(pl.multiple_of(offset, ROW_VREGS), ROW_VREGS)


def _moe_kernel(pos_s, wgt_s, seg_s,
                htm_ref, x_ref, gt_ref, fg_ref, wg_hbm, wu_hbm, wd_hbm, o_ref,
                tm_ref, srt_ref, wgu_buf, wd_buf, sem, *, final_norm):
    tm, tb, nb, ch = MOE_TILE, MOE_BLOCK, MOE_BLOCKS, MOE_CHUNK
    i = pl.program_id(0)
    j = pl.program_id(1)
    base = i * (TOP_K * tm) + (j % nb) * tb

    def weight_copies(e, slot):
        return (pltpu.make_async_copy(wg_hbm.at[e], wgu_buf.at[slot, :, 0:D_EXPERT], sem.at[0, slot]),
                pltpu.make_async_copy(wu_hbm.at[e], wgu_buf.at[slot, :, D_EXPERT:], sem.at[1, slot]),
                pltpu.make_async_copy(wd_hbm.at[e], wd_buf.at[slot], sem.at[2, slot]))

    @pl.when(j == 0)
    def _():
        for e0 in range(W_SLOTS - 1):
            for cp in weight_copies(e0, e0):
                cp.start()
        srt_ref[TOP_K * tm * ROW_VREGS:, :] = jnp.zeros((ch * ROW_VREGS, LANES), _F32)

    @pl.when(j < nb)
    def _():
        def dispatch(it, carry):
            t0 = it * UNROLL
            rows = [htm_ref[_row_ds((t0 + u) * ROW_VREGS), :] for u in range(UNROLL)]
            for u in range(UNROLL):
                for k in range(TOP_K):
                    srt_ref[_row_ds(pos_s[base + k * tm + t0 + u]), :] = rows[u]
            return carry

        lax.fori_loop(0, tb // UNROLL, dispatch, 0)

    @pl.when(j == nb)
    def _():
        def expert(e, carry):
            slot = e % W_SLOTS
            for cp in weight_copies(e, slot):
                cp.wait()

            @pl.when(e + W_SLOTS - 1 < N_EXPERTS)
            def _():
                for cp in weight_copies(e + W_SLOTS - 1, (e + W_SLOTS - 1) % W_SLOTS):
                    cp.start()

            off = seg_s[i * (2 * N_EXPERTS) + e]
            cnt = seg_s[i * (2 * N_EXPERTS) + N_EXPERTS + e]

            def run_rows(first_row, n_sub):
                subs = range(n_sub)
                row0 = [first_row + sub * MOE_SUB for sub in subs]
                r0 = [(off + r) * ROW_VREGS for r in row0]
                xs = [jnp.concatenate([srt_ref[pl.ds(r0[sub] + s, MOE_SUB, stride=ROW_VREGS), :]
                                       for s in range(ROW_VREGS)], axis=1) for sub in subs]
                ys = []
                hgus = [jnp.dot(xs[sub].astype(_BF16), wgu_buf[slot], preferred_element_type=_F32)
                        for sub in subs]
                for sub in subs:
                    hgu = hgus[sub]
                    act = _silu(hgu[:, :D_EXPERT]) * hgu[:, D_EXPERT:]
                    y = jnp.dot(act.astype(_BF16), wd_buf[slot], preferred_element_type=_F32)
                    valid = lax.broadcasted_iota(jnp.int32, (MOE_SUB, 1), 0) < (cnt - row0[sub])
                    ys.append(jnp.where(valid, y, xs[sub]))
                for sub in subs:
                    for s in range(ROW_VREGS):
                        srt_ref[pl.ds(r0[sub] + s, MOE_SUB, stride=ROW_VREGS), :] = (
                            ys[sub][:, s * LANES:(s + 1) * LANES])

            def chunk(c, inner):
                run_rows(c * ch, ch // MOE_SUB)
                return inner

            n_whole = cnt // ch
            lax.fori_loop(0, n_whole, chunk, 0)
            rest = cnt - n_whole * ch
            for n_sub in range(1, ch // MOE_SUB + 1):
                @pl.when((rest > (n_sub - 1) * MOE_SUB) & (rest <= n_sub * MOE_SUB))
                def _(n_sub=n_sub):
                    run_rows(n_whole * ch, n_sub)
            return carry

        lax.fori_loop(0, N_EXPERTS, expert, 0)

    @pl.when(j >= nb)
    def _():
        def combine(it, carry):
            t0 = it * UNROLL
            outs = []
            for u in range(UNROLL):
                t = base + t0 + u
                y0 = srt_ref[_row_ds(pos_s[t]), :]
                y1 = srt_ref[_row_ds(pos_s[t + tm]), :]
                outs.append(wgt_s[t] * y0 + wgt_s[t + tm] * y1)
            for u in range(UNROLL):
                tm_ref[_row_ds((t0 + u) * ROW_VREGS), :] = outs[u]
            return carry

        lax.fori_loop(0, tb // UNROLL, combine, 0)

        moe = jnp.concatenate(
            [tm_ref[pl.ds(s, tb, stride=ROW_VREGS), :] for s in range(ROW_VREGS)], axis=1)
        y = x_ref[...] + gt_ref[0] * moe
        if final_norm:
            y = _rms_norm(y, fg_ref[...])
        o_ref[...] = y


def _moe(x, sh, sc, gt, norm_g, w_rt, b_rt, w_g, w_u, w_d, final_g, final_norm):
    b, s, d = x.shape
    t = b * s
    tm, tb, nb = MOE_TILE, MOE_BLOCK, MOE_BLOCKS
    per_b = s // tm
    n_tiles = t // tm
    xf = x.reshape(t, d)
    h_tm, pos, wgt, seg = _route(xf, sh, sc, norm_g, w_rt, b_rt, per_b)
    const2 = lambda i, j, *_: (0, 0)
    mod_spec = pl.BlockSpec((1, 1, d), lambda i, j, *_: (i // per_b, 0, 0))
    out = pl.pallas_call(
        functools.partial(_moe_kernel, final_norm=final_norm),
        grid_spec=pltpu.PrefetchScalarGridSpec(
            num_scalar_prefetch=3,
            grid=(n_tiles, 2 * nb),
            in_specs=[
                pl.BlockSpec((tb * ROW_VREGS, LANES),
                             lambda i, j, *_: (i * nb + jnp.minimum(j, nb - 1), 0)),
                pl.BlockSpec((tb, d), lambda i, j, *_: (i * nb + jnp.maximum(j - nb, 0), 0)),
                mod_spec,
                pl.BlockSpec((1, d), const2),
                pl.BlockSpec(memory_space=pl.ANY),
                pl.BlockSpec(memory_space=pl.ANY),
                pl.BlockSpec(memory_space=pl.ANY),
            ],
            out_specs=pl.BlockSpec((tb, d), lambda i, j, *_: (i * nb + jnp.maximum(j - nb, 0), 0)),
            scratch_shapes=[
                pltpu.VMEM((tb * ROW_VREGS, LANES), _F32),
                pltpu.VMEM(((TOP_K * tm + MOE_CHUNK) * ROW_VREGS, LANES), _F32),
                pltpu.VMEM((W_SLOTS, d, 2 * D_EXPERT), _BF16),
                pltpu.VMEM((W_SLOTS, D_EXPERT, d), _BF16),
                pltpu.SemaphoreType.DMA((3, W_SLOTS)),
            ],
        ),
        out_shape=jax.ShapeDtypeStruct((t, d), _F32),
        compiler_params=pltpu.CompilerParams(
            dimension_semantics=("arbitrary", "arbitrary"), vmem_limit_bytes=VMEM_LIMIT),
        name="moe",
    )(pos.reshape(-1), wgt.reshape(-1), seg[:, :, 0].reshape(-1),
      h_tm, xf, gt, final_g.reshape(1, d), w_g, w_u, w_d)
    return out.reshape(b, s, d)


def _router_params(w_rg, b_rg, w_re, b_re):
    d = w_rg.shape[0]
    pad = ROUTER_ROWS - N_EXPERTS - N_GROUPS
    w = jnp.concatenate([w_re.T, w_rg.T, jnp.zeros((pad, d), _F32)], axis=0)
    b = jnp.concatenate([b_re, b_rg, jnp.zeros((pad,), _F32)]).reshape(ROUTER_ROWS, 1)
    return w, b


def kernel(x, c, mixer_norm_g, w_ada, b_ada, w_in, conv_w, conv_b, conv_ln_g, conv_ln_b, w_conv_out,
           b_conv_out, pool_w, pool_scale, w_out, ffn_norm_g, w_router_group, b_router_group,
           w_router_expert, b_router_expert, w_expert_gate, w_expert_up, w_expert_down, final_norm_g):
    depth = w_ada.shape[0]
    bsz = x.shape[0]
    mod = _ada(c, w_ada, b_ada)
    mod = mod.reshape(depth, bsz, N_MOD, 1, D_MODEL)
    for l in range(depth):
        sh1, sc1, g1, sh2, sc2, g2 = (mod[l, :, i] for i in range(N_MOD))
        x = _mixer(x, sh1, sc1, g1, mixer_norm_g[l], w_in[l].astype(_BF16), conv_w[l], conv_b[l],
                   conv_ln_g[l], conv_ln_b[l], w_conv_out[l].astype(_BF16), b_conv_out[l],
                   pool_w[l].astype(_BF16), pool_scale[l], w_out[l].astype(_BF16))
        w_rt, b_rt = _router_params(w_router_group[l], b_router_group[l],
                                    w_router_expert[l], b_router_expert[l])
        x = _moe(x, sh2, sc2, g2, ffn_norm_g[l], w_rt, b_rt, w_expert_gate[l].astype(_BF16),
                 w_expert_up[l].astype(_BF16), w_expert_down[l].astype(_BF16),
                 final_norm_g, final_norm=(l == depth - 1))
    return x
```

```python
import functools

import jax
import jax.numpy as jnp
from jax import lax
from jax.experimental import pallas as pl
from jax.experimental.pallas import tpu as pltpu

D_MODEL = 1024
D_CONV = 512
D_POOL = 512
CONV_K = 31
POOL_WINDOWS = (2, 4, 8, 16)
POOL_GROUP = 128
POOL_OUT_GROUP = 256
N_GROUPS = 4
PER_GROUP = 8
N_EXPERTS = 32
TOP_K = 2
D_EXPERT = 256
N_MOD = 6
EPS = 1e-6

COL_POOL = 2 * D_CONV
COL_GATE_A = COL_POOL + D_POOL
COL_GATE_B = COL_GATE_A + D_MODEL
D_IN = COL_GATE_B + D_MODEL

LANES = 128
SUBLANES = 8
ROW_VREGS = D_MODEL // LANES
HALO = 32
SEQ_TILE = 512
CONV_ROWS = 128
CONV_ROW_STEPS = SEQ_TILE // CONV_ROWS
CONV_LANE_GROUPS = D_CONV // LANES
CONV_ACCS = 2
TAIL_ROWS = 128
GATE_COLS = 256
GATE_COL_BLOCKS = (D_IN - COL_GATE_A) // GATE_COLS
MOE_TILE = 4096
MOE_BLOCK = 512
MOE_BLOCKS = MOE_TILE // MOE_BLOCK
ROUTE_BLOCK = 1024
ROUTE_BLOCKS = MOE_TILE // ROUTE_BLOCK
MOE_CHUNK = 256
MOE_SUB = 128
W_SLOTS = 4
CUM_BLOCK = 256
ROUTER_ROWS = 48
GROUP_ROW0 = N_EXPERTS
UNROLL = 8
VMEM_LIMIT = 56 * 1024 * 1024

_F32 = jnp.float32
_BF16 = jnp.bfloat16
assert ROW_VREGS == SUBLANES


def _sigmoid(v):
    return 1.0 / (1.0 + jnp.exp(-v))


def _silu(v):
    return v * _sigmoid(v)


def _rms_norm(v, g):
    return v * lax.rsqrt(jnp.mean(v * v, axis=-1, keepdims=True) + EPS) * g


def _ada_kernel(c_ref, w_ref, b_ref, o_ref):
    c_act = _silu(c_ref[...])
    o_ref[0] = jnp.dot(c_act, w_ref[0], preferred_element_type=_F32,
                       precision=lax.Precision.HIGHEST) + b_ref[0]


def _ada(c, w_ada, b_ada):
    depth, d, n = w_ada.shape
    tn = 1024
    return pl.pallas_call(
        _ada_kernel,
        grid=(depth, n // tn),
        in_specs=[
            pl.BlockSpec(c.shape, lambda l, j: (0, 0)),
            pl.BlockSpec((1, d, tn), lambda l, j: (l, 0, j)),
            pl.BlockSpec((1, 1, tn), lambda l, j: (l, 0, j)),
        ],
        out_specs=pl.BlockSpec((1, c.shape[0], tn), lambda l, j: (l, 0, j)),
        out_shape=jax.ShapeDtypeStruct((depth, c.shape[0], n), _F32),
        name="ada_mod",
    )(c, w_ada, b_ada.reshape(depth, 1, n))


def _mixer_kernel(x_ref, sh_ref, sc_ref, gt_ref, ng_ref, win_ref, cw_ref, cb_ref, lng_ref, lnb_ref,
                  wco_ref, bco_ref, pw_ref, ps_ref, wout_ref, o_ref,
                  gates_ref, e_ref, s_ref, conv_ref, pe_ref, p2_ref, p4_ref, p8_ref):
    ts = SEQ_TILE
    j = pl.program_id(1)

    @pl.when(j == 0)
    def _():
        e_ref[:, 0:HALO, :] = jnp.zeros((CONV_LANE_GROUPS, HALO, LANES), _F32)
        pe_ref[0:HALO, :] = jnp.zeros((HALO, D_POOL), _F32)

    @pl.when(j > 0)
    def _():
        e_ref[:, 0:HALO, :] = e_ref[:, ts:ts + HALO, :]
        pe_ref[0:HALO, :] = pe_ref[ts:ts + HALO, :]

    xt = x_ref[0]
    h = _rms_norm(xt, ng_ref[...]) * (1.0 + sc_ref[0]) + sh_ref[0]
    hb = h.astype(_BF16)

    a = jnp.dot(hb, win_ref[:, 0:D_CONV], preferred_element_type=_F32)
    g = jnp.dot(hb, win_ref[:, D_CONV:2 * D_CONV], preferred_element_type=_F32)
    glu = a * _sigmoid(g)
    for c in range(CONV_LANE_GROUPS):
        e_ref[c, HALO:HALO + ts, :] = glu[:, c * LANES:(c + 1) * LANES]
        for sft in range(1, SUBLANES):
            s_ref[sft - 1, c] = e_ref[c, sft:sft + ts + HALO - SUBLANES, :]

    def conv_block(c, r0, zero_row):
        accs = [jnp.broadcast_to(cb_ref[c] + zero_row, (CONV_ROWS, LANES))] + [None] * (CONV_ACCS - 1)
        for k in range(CONV_K):
            q, sft = divmod(HALO - (CONV_K - 1) + k, SUBLANES)
            lo = r0 + SUBLANES * q
            tap = e_ref[c, lo:lo + CONV_ROWS, :] if sft == 0 else s_ref[sft - 1, c, lo:lo + CONV_ROWS, :]
            prod = tap * cw_ref[k, c]
            a_i = k % CONV_ACCS
            accs[a_i] = prod if accs[a_i] is None else accs[a_i] + prod
        while len(accs) > 1:
            accs = [accs[p] + accs[p + 1] for p in range(0, len(accs), 2)]
        conv_ref[c, r0:r0 + CONV_ROWS, :] = accs[0]

    n_conv = CONV_LANE_GROUPS * CONV_ROW_STEPS
    convs_per_gate = n_conv // GATE_COL_BLOCKS
    for it in range(n_conv):
        if it % convs_per_gate == 0:
            n = it // convs_per_gate
            col = COL_GATE_A + n * GATE_COLS
            gate = _sigmoid(jnp.dot(hb, win_ref[:, col:col + GATE_COLS], preferred_element_type=_F32))
            gates_ref[n] = gate
            zero_row = jnp.where(gate[ts - 1:ts, GATE_COLS - LANES:] > 2.0, 1.0, 0.0)
        conv_block(it // CONV_ROW_STEPS, (it % CONV_ROW_STEPS) * CONV_ROWS, zero_row)

    u = jnp.dot(hb, win_ref[:, COL_POOL:COL_GATE_A], preferred_element_type=_F32)
    pe_ref[HALO:HALO + ts, :] = u
    n_ext = ts + HALO
    p2_ref[8:n_ext, :] = pe_ref[8:n_ext, :] + pe_ref[7:n_ext - 1, :]
    p4_ref[16:n_ext, :] = p2_ref[16:n_ext, :] + p2_ref[14:n_ext - 2, :]
    p8_ref[24:n_ext, :] = p4_ref[24:n_ext, :] + p4_ref[20:n_ext - 4, :]

    half = GATE_COL_BLOCKS // 2
    for rb in range(ts // TAIL_ROWS):
        r0 = rb * TAIL_ROWS
        rows = slice(r0, r0 + TAIL_ROWS)
        ext = slice(HALO + r0, HALO + r0 + TAIL_ROWS)
        v = jnp.concatenate([conv_ref[c, rows, :] for c in range(CONV_LANE_GROUPS)], axis=1)
        mu = jnp.mean(v, axis=-1, keepdims=True)
        vc = v - mu
        var = jnp.mean(vc * vc, axis=-1, keepdims=True)
        v = _silu(vc * lax.rsqrt(var + EPS) * lng_ref[...] + lnb_ref[...])
        y_a = jnp.dot(v.astype(_BF16), wco_ref[...], preferred_element_type=_F32) + bco_ref[...]

        pos = j * ts + r0 + lax.broadcasted_iota(jnp.int32, (TAIL_ROWS, 1), 0) + 1
        y_b_parts = []
        for gi, w in enumerate(POOL_WINDOWS):
            cs = slice(gi * POOL_GROUP, (gi + 1) * POOL_GROUP)
            if w == 2:
                win_sum = p2_ref[ext, cs]
            elif w == 4:
                win_sum = p4_ref[ext, cs]
            elif w == 8:
                win_sum = p8_ref[ext, cs]
            else:
                win_sum = p8_ref[ext, cs] + p8_ref[HALO + r0 - 8:HALO + r0 - 8 + TAIL_ROWS, cs]
            count = jnp.minimum(pos, w).astype(_F32)
            pooled = win_sum / count - pe_ref[ext, cs]
            y_b_parts.append(jnp.dot(pooled.astype(_BF16), pw_ref[gi], preferred_element_type=_F32))
        y_b = jnp.concatenate(y_b_parts, axis=-1) * ps_ref[...]

        gate_a = jnp.concatenate([gates_ref[n, rows, :] for n in range(half)], axis=1)
        gate_b = jnp.concatenate([gates_ref[n, rows, :] for n in range(half, GATE_COL_BLOCKS)], axis=1)
        mixed = gate_a * y_a + gate_b * y_b
        out = jnp.dot(mixed.astype(_BF16), wout_ref[...], preferred_element_type=_F32)
        o_ref[0, rows, :] = x_ref[0, rows, :] + gt_ref[0] * out


def _mixer(x, sh, sc, gt, layer, norm_g, w_in, conv_w, conv_b, ln_g, ln_b, w_co, b_co, pool_w, pool_scale,
           w_out):
    b, s, d = x.shape
    ts = SEQ_TILE
    const2 = lambda bi, j: (0, 0)
    const3 = lambda bi, j: (0, 0, 0)
    row = lambda a: a.reshape(1, -1)

    def layer_spec(a):
        return pl.BlockSpec((None,) + a.shape[1:], lambda bi, j: (layer,) + (0,) * (a.ndim - 1))
    mod_spec = pl.BlockSpec((1, 1, d), lambda bi, j: (bi, 0, 0))
    conv_w = conv_w.reshape(CONV_K, CONV_LANE_GROUPS, 1, LANES)
    conv_b = conv_b.reshape(CONV_LANE_GROUPS, 1, LANES)
    return pl.pallas_call(
        _mixer_kernel,
        grid=(b, s // ts),
        in_specs=[
            pl.BlockSpec((1, ts, d), lambda bi, j: (bi, j, 0)),
            mod_spec, mod_spec, mod_spec,
            pl.BlockSpec((1, d), const2),
            layer_spec(w_in),
            pl.BlockSpec(conv_w.shape, lambda bi, j: (0, 0, 0, 0)),
            pl.BlockSpec(conv_b.shape, const3),
            pl.BlockSpec((1, D_CONV), const2),
            pl.BlockSpec((1, D_CONV), const2),
            layer_spec(w_co),
            pl.BlockSpec((1, d), const2),
            layer_spec(pool_w),
            pl.BlockSpec((1, d), const2),
            layer_spec(w_out),
        ],
        out_specs=pl.BlockSpec((1, ts, d), lambda bi, j: (bi, j, 0)),
        out_shape=jax.ShapeDtypeStruct(x.shape, _F32),
        scratch_shapes=[
            pltpu.VMEM((GATE_COL_BLOCKS, ts, GATE_COLS), _F32),
            pltpu.VMEM((CONV_LANE_GROUPS, ts + HALO, LANES), _F32),
            pltpu.VMEM((SUBLANES - 1, CONV_LANE_GROUPS, ts + HALO - SUBLANES, LANES), _F32),
            pltpu.VMEM((CONV_LANE_GROUPS, ts, LANES), _F32),
            pltpu.VMEM((ts + HALO, D_POOL), _F32),
            pltpu.VMEM((ts + HALO, D_POOL), _F32),
            pltpu.VMEM((ts + HALO, D_POOL), _F32),
            pltpu.VMEM((ts + HALO, D_POOL), _F32),
        ],
        compiler_params=pltpu.CompilerParams(
            dimension_semantics=("arbitrary", "arbitrary"), vmem_limit_bytes=VMEM_LIMIT),
        name="mixer",
    )(x, sh, sc, gt, row(norm_g), w_in, conv_w, conv_b, row(ln_g), row(ln_b), w_co, row(b_co),
      pool_w, row(pool_scale), w_out)


def _route_kernel(x_ref, sh_ref, sc_ref, ng_ref, wrt_ref, brt_ref, pos_ref, wgt_ref, seg_ref,
                  id_scr, rank_scr, wgt_scr, run_ref):
    tb = ROUTE_BLOCK
    j = pl.program_id(1)

    @pl.when(j == 0)
    def _():
        run_ref[...] = jnp.zeros_like(run_ref)

    h = _rms_norm(x_ref[...], ng_ref[...]) * (1.0 + sc_ref[0]) + sh_ref[0]
    nt = (((1,), (1,)), ((), ()))
    h_hi = h.astype(_BF16)
    h_lo = (h - h_hi.astype(_F32)).astype(_BF16)
    w = wrt_ref[...]
    w_hi = w.astype(_BF16)
    w_lo = (w - w_hi.astype(_F32)).astype(_BF16)
    part = lax.dot_general(jnp.concatenate([w_hi, w_lo], axis=0), h_hi, nt, preferred_element_type=_F32)
    logits = (part[0:ROUTER_ROWS] + part[ROUTER_ROWS:]
              + lax.dot_general(w_hi, h_lo, nt, preferred_element_type=_F32)
              + brt_ref[...])
    neg = jnp.float32(-jnp.inf)
    big = jnp.int32(N_EXPERTS)

    grow = lax.broadcasted_iota(jnp.int32, (SUBLANES, tb), 0)
    glog = jnp.where(grow < N_GROUPS, logits[GROUP_ROW0:GROUP_ROW0 + SUBLANES, :], neg)
    gmax = jnp.max(glog, axis=0, keepdims=True)
    p_group = 1.0 / jnp.sum(jnp.exp(glog - gmax), axis=0, keepdims=True)
    g_idx = jnp.min(jnp.where(glog == gmax, grow, big), axis=0, keepdims=True)

    erow = lax.broadcasted_iota(jnp.int32, (N_EXPERTS, tb), 0)
    el = jnp.where((erow // PER_GROUP) == g_idx, logits[0:N_EXPERTS, :], neg)
    m1 = jnp.max(el, axis=0, keepdims=True)
    i1 = jnp.min(jnp.where(el == m1, erow, big), axis=0, keepdims=True)
    el2 = jnp.where(erow == i1, neg, el)
    m2 = jnp.max(el2, axis=0, keepdims=True)
    i2 = jnp.min(jnp.where(el2 == m2, erow, big), axis=0, keepdims=True)
    r = jnp.exp(m2 - m1)
    w1 = p_group / (1.0 + r)
    w2 = w1 * r
    sel1 = erow == i1
    sel2 = erow == i2

    used = jnp.where(sel1 | sel2, 1.0, 0.0)
    tri_r = lax.broadcasted_iota(jnp.int32, (CUM_BLOCK, CUM_BLOCK), 0)
    tri_c = lax.broadcasted_iota(jnp.int32, (CUM_BLOCK, CUM_BLOCK), 1)
    upper = jnp.where(tri_r < tri_c, 1.0, 0.0).astype(_BF16)
    run = run_ref[:, 0:1]
    ranks = []
    for blk in range(tb // CUM_BLOCK):
        ub = used[:, blk * CUM_BLOCK:(blk + 1) * CUM_BLOCK]
        ranks.append(jnp.dot(ub.astype(_BF16), upper, preferred_element_type=_F32) + run)
        run = run + jnp.sum(ub, axis=1, keepdims=True)
    rank = jnp.concatenate(ranks, axis=1)
    run_ref[...] = jnp.broadcast_to(run, run_ref.shape)
    id_scr[j, 0:1, :] = i1
    id_scr[j, 1:2, :] = i2
    rank_scr[j, 0:1, :] = jnp.sum(jnp.where(sel1, rank, 0.0), axis=0, keepdims=True)
    rank_scr[j, 1:2, :] = jnp.sum(jnp.where(sel2, rank, 0.0), axis=0, keepdims=True)
    wgt_scr[j, 0:1, :] = w1
    wgt_scr[j, 1:2, :] = w2

    @pl.when(j == ROUTE_BLOCKS - 1)
    def _():
        cnt = jnp.concatenate([run_ref[...], jnp.zeros((LANES - N_EXPERTS, LANES), _F32)], axis=0)
        low_r = lax.broadcasted_iota(jnp.int32, (LANES, LANES), 0)
        low_c = lax.broadcasted_iota(jnp.int32, (LANES, LANES), 1)
        lower = jnp.where(low_c < low_r, 1.0, 0.0)
        off = jnp.dot(lower, cnt, preferred_element_type=_F32, precision=lax.Precision.HIGHEST)
        seg_ref[0, 0:N_EXPERTS, :] = off[0:N_EXPERTS, :].astype(jnp.int32)
        seg_ref[0, N_EXPERTS:2 * N_EXPERTS, :] = cnt[0:N_EXPERTS, :].astype(jnp.int32)
        off_col = off[0:N_EXPERTS, 0:1]
        for blk in range(ROUTE_BLOCKS):
            ls = slice(blk * tb, (blk + 1) * tb)
            for k in range(TOP_K):
                start = jnp.sum(jnp.where(erow == id_scr[blk, k:k + 1, :], off_col, 0.0),
                                axis=0, keepdims=True)
                slot = start + rank_scr[blk, k:k + 1, :]
                pos_ref[0, k:k + 1, ls] = (slot * ROW_VREGS).astype(jnp.int32)
                wgt_ref[0, k:k + 1, ls] = wgt_scr[blk, k:k + 1, :]


def _route(xf, sh, sc, norm_g, w_rt, b_rt, per_b):
    t, d = xf.shape
    tm, tb, nb = MOE_TILE, ROUTE_BLOCK, ROUTE_BLOCKS
    n_tiles = t // tm
    const2 = lambda i, j: (0, 0)
    mod_spec = pl.BlockSpec((1, 1, d), lambda i, j: (i // per_b, 0, 0))
    out3 = lambda i, j: (i, 0, 0)
    return pl.pallas_call(
        _route_kernel,
        grid=(n_tiles, nb),
        in_specs=[
            pl.BlockSpec((tb, d), lambda i, j: (i * nb + j, 0)),
            mod_spec, mod_spec,
            pl.BlockSpec((1, d), const2),
            pl.BlockSpec((ROUTER_ROWS, d), const2),
            pl.BlockSpec((ROUTER_ROWS, 1), const2),
        ],
        out_specs=[
            pl.BlockSpec((1, TOP_K, tm), out3),
            pl.BlockSpec((1, TOP_K, tm), out3),
            pl.BlockSpec((1, 2 * N_EXPERTS, LANES), out3),
        ],
        out_shape=[
            jax.ShapeDtypeStruct((n_tiles, TOP_K, tm), jnp.int32),
            jax.ShapeDtypeStruct((n_tiles, TOP_K, tm), _F32),
            jax.ShapeDtypeStruct((n_tiles, 2 * N_EXPERTS, LANES), jnp.int32),
        ],
        scratch_shapes=[
            pltpu.VMEM((nb, TOP_K, tb), jnp.int32),
            pltpu.VMEM((nb, TOP_K, tb), _F32),
            pltpu.VMEM((nb, TOP_K, tb), _F32),
            pltpu.VMEM((N_EXPERTS, LANES), _F32),
        ],
        compiler_params=pltpu.CompilerParams(
            dimension_semantics=("arbitrary", "arbitrary"), vmem_limit_bytes=VMEM_LIMIT),
        name="route",
    )(xf, sh, sc, norm_g.reshape(1, d), w_rt, b_rt)


def _row_ds(offset):
    return pl.ds(pl.multiple_of(offset, ROW_VREGS), ROW_VREGS)


def _moe_kernel(pos_s, wgt_s, seg_s,
                x_ref, sh_ref, sc_ref, gt_ref, ng_ref, fg_ref, wg_hbm, wu_hbm, wd_hbm, o_ref,
                tm_ref, srt_ref, wgu_buf, wd_buf, sem, *, layer, final_norm):
    tm, tb, nb, ch = MOE_TILE, MOE_BLOCK, MOE_BLOCKS, MOE_CHUNK
    i = pl.program_id(0)
    j = pl.program_id(1)
    base = i * (TOP_K * tm) + (j % nb) * tb

    def weight_copies(e, slot):
        return (pltpu.make_async_copy(wg_hbm.at[layer, e], wgu_buf.at[slot, :, 0:D_EXPERT], sem.at[0, slot]),
                pltpu.make_async_copy(wu_hbm.at[layer, e], wgu_buf.at[slot, :, D_EXPERT:], sem.at[1, slot]),
                pltpu.make_async_copy(wd_hbm.at[layer, e], wd_buf.at[slot], sem.at[2, slot]))

    @pl.when(j == 0)
    def _():
        for e0 in range(W_SLOTS - 1):
            for cp in weight_copies(e0, e0):
                cp.start()
        srt_ref[TOP_K * tm * ROW_VREGS:, :] = jnp.zeros((ch * ROW_VREGS, LANES), _F32)

    @pl.when(j < nb)
    def _():
        h = _rms_norm(x_ref[...], ng_ref[...]) * (1.0 + sc_ref[0]) + sh_ref[0]
        for s in range(ROW_VREGS):
            tm_ref[pl.ds(s, tb, stride=ROW_VREGS), :] = h[:, s * LANES:(s + 1) * LANES]

        def dispatch(it, carry):
            t0 = it * UNROLL
            rows = [tm_ref[_row_ds((t0 + u) * ROW_VREGS), :] for u in range(UNROLL)]
            for u in range(UNROLL):
                for k in range(TOP_K):
                    srt_ref[_row_ds(pos_s[base + k * tm + t0 + u]), :] = rows[u]
            return carry

        lax.fori_loop(0, tb // UNROLL, dispatch, 0)

    @pl.when(j == nb)
    def _():
        def expert(e, carry):
            slot = e % W_SLOTS
            for cp in weight_copies(e, slot):
                cp.wait()

            @pl.when(e + W_SLOTS - 1 < N_EXPERTS)
            def _():
                for cp in weight_copies(e + W_SLOTS - 1, (e + W_SLOTS - 1) % W_SLOTS):
                    cp.start()

            off = seg_s[i * (2 * N_EXPERTS) + e]
            cnt = seg_s[i * (2 * N_EXPERTS) + N_EXPERTS + e]

            def run_rows(first_row, n_sub):
                subs = range(n_sub)
                row0 = [first_row + sub * MOE_SUB for sub in subs]
                r0 = [(off + r) * ROW_VREGS for r in row0]
                xs = [jnp.concatenate([srt_ref[pl.ds(r0[sub] + s, MOE_SUB, stride=ROW_VREGS), :]
                                       for s in range(ROW_VREGS)], axis=1) for sub in subs]
                ys = []
                hgus = [jnp.dot(xs[sub].astype(_BF16), wgu_buf[slot], preferred_element_type=_F32)
                        for sub in subs]
                for sub in subs:
                    hgu = hgus[sub]
                    act = _silu(hgu[:, :D_EXPERT]) * hgu[:, D_EXPERT:]
                    y = jnp.dot(act.astype(_BF16), wd_buf[slot], preferred_element_type=_F32)
                    valid = lax.broadcasted_iota(jnp.int32, (MOE_SUB, 1), 0) < (cnt - row0[sub])
                    ys.append(jnp.where(valid, y, xs[sub]))
                for sub in subs:
                    for s in range(ROW_VREGS):
                        srt_ref[pl.ds(r0[sub] + s, MOE_SUB, stride=ROW_VREGS), :] = (
                            ys[sub][:, s * LANES:(s + 1) * LANES])

            def chunk(c, inner):
                run_rows(c * ch, ch // MOE_SUB)
                return inner

            n_whole = cnt // ch
            lax.fori_loop(0, n_whole, chunk, 0)
            rest = cnt - n_whole * ch
            for n_sub in range(1, ch // MOE_SUB + 1):
                @pl.when((rest > (n_sub - 1) * MOE_SUB) & (rest <= n_sub * MOE_SUB))
                def _(n_sub=n_sub):
                    run_rows(n_whole * ch, n_sub)
            return carry

        lax.fori_loop(0, N_EXPERTS, expert, 0)

    @pl.when(j >= nb)
    def _():
        def combine(it, carry):
            t0 = it * UNROLL
            outs = []
            for u in range(UNROLL):
                t = base + t0 + u
                y0 = srt_ref[_row_ds(pos_s[t]), :]
                y1 = srt_ref[_row_ds(pos_s[t + tm]), :]
                outs.append(wgt_s[t] * y0 + wgt_s[t + tm] * y1)
            for u in range(UNROLL):
                tm_ref[_row_ds((t0 + u) * ROW_VREGS), :] = outs[u]
            return carry

        lax.fori_loop(0, tb // UNROLL, combine, 0)

        moe = jnp.concatenate(
            [tm_ref[pl.ds(s, tb, stride=ROW_VREGS), :] for s in range(ROW_VREGS)], axis=1)
        y = x_ref[...] + gt_ref[0] * moe
        if final_norm:
            y = _rms_norm(y, fg_ref[...])
        o_ref[...] = y


def _moe(x, sh, sc, gt, layer, norm_g, w_rt, b_rt, w_g, w_u, w_d, final_g, final_norm):
    b, s, d = x.shape
    t = b * s
    tm, tb, nb = MOE_TILE, MOE_BLOCK, MOE_BLOCKS
    per_b = s // tm
    n_tiles = t // tm
    xf = x.reshape(t, d)
    pos, wgt, seg = _route(xf, sh, sc, norm_g, w_rt, b_rt, per_b)
    const2 = lambda i, j, *_: (0, 0)
    mod_spec = pl.BlockSpec((1, 1, d), lambda i, j, *_: (i // per_b, 0, 0))
    out = pl.pallas_call(
        functools.partial(_moe_kernel, layer=layer, final_norm=final_norm),
        grid_spec=pltpu.PrefetchScalarGridSpec(
            num_scalar_prefetch=3,
            grid=(n_tiles, 2 * nb),
            in_specs=[
                pl.BlockSpec((tb, d), lambda i, j, *_: (i * nb + j % nb, 0)),
                mod_spec, mod_spec, mod_spec,
                pl.BlockSpec((1, d), const2),
                pl.BlockSpec((1, d), const2),
                pl.BlockSpec(memory_space=pl.ANY),
                pl.BlockSpec(memory_space=pl.ANY),
                pl.BlockSpec(memory_space=pl.ANY),
            ],
            out_specs=pl.BlockSpec((tb, d), lambda i, j, *_: (i * nb + jnp.maximum(j - nb, 0), 0)),
            scratch_shapes=[
                pltpu.VMEM((tb * ROW_VREGS, LANES), _F32),
                pltpu.VMEM(((TOP_K * tm + MOE_CHUNK) * ROW_VREGS, LANES), _F32),
                pltpu.VMEM((W_SLOTS, d, 2 * D_EXPERT), _BF16),
                pltpu.VMEM((W_SLOTS, D_EXPERT, d), _BF16),
                pltpu.SemaphoreType.DMA((3, W_SLOTS)),
            ],
        ),
        out_shape=jax.ShapeDtypeStruct((t, d), _F32),
        compiler_params=pltpu.CompilerParams(
            dimension_semantics=("arbitrary", "arbitrary"), vmem_limit_bytes=VMEM_LIMIT),
        name="moe",
    )(pos.reshape(-1), wgt.reshape(-1), seg[:, :, 0].reshape(-1),
      xf, sh, sc, gt, norm_g.reshape(1, d), final_g.reshape(1, d), w_g, w_u, w_d)
    return out.reshape(b, s, d)


def _router_params(w_rg, b_rg, w_re, b_re):
    d = w_rg.shape[0]
    pad = ROUTER_ROWS - N_EXPERTS - N_GROUPS
    w = jnp.concatenate([w_re.T, w_rg.T, jnp.zeros((pad, d), _F32)], axis=0)
    b = jnp.concatenate([b_re, b_rg, jnp.zeros((pad,), _F32)]).reshape(ROUTER_ROWS, 1)
    return w, b


def kernel(x, c, mixer_norm_g, w_ada, b_ada, w_in, conv_w, conv_b, conv_ln_g, conv_ln_b, w_conv_out,
           b_conv_out, pool_w, pool_scale, w_out, ffn_norm_g, w_router_group, b_router_group,
           w_router_expert, b_router_expert, w_expert_gate, w_expert_up, w_expert_down, final_norm_g):
    depth = w_ada.shape[0]
    bsz = x.shape[0]
    mod = _ada(c, w_ada, b_ada)
    mod = mod.reshape(depth, bsz, N_MOD, 1, D_MODEL)
    w_in_b, w_co_b, pool_w_b, w_out_b = (a.astype(_BF16) for a in (w_in, w_conv_out, pool_w, w_out))
    w_eg_b, w_eu_b, w_ed_b = (a.astype(_BF16) for a in (w_expert_gate, w_expert_up, w_expert_down))
    for l in range(depth):
        sh1, sc1, g1, sh2, sc2, g2 = (mod[l, :, i] for i in range(N_MOD))
        x = _mixer(x, sh1, sc1, g1, l, mixer_norm_g[l], w_in_b, conv_w[l], conv_b[l],
                   conv_ln_g[l], conv_ln_b[l], w_co_b, b_conv_out[l], pool_w_b, pool_scale[l], w_out_b)
        w_rt, b_rt = _router_params(w_router_group[l], b_router_group[l],
                                    w_router_expert[l], b_router_expert[l])
        x = _moe(x, sh2, sc2, g2, l, ffn_norm_g[l], w_rt, b_rt, w_eg_b, w_eu_b, w_ed_b,
                 final_norm_g, final_norm=(l == depth - 1))
    return x
```

```python
import functools

import jax
import jax.numpy as jnp
from jax import lax
from jax.experimental import pallas as pl
from jax.experimental.pallas import tpu as pltpu

D_MODEL = 1024
D_CONV = 512
D_POOL = 512
CONV_K = 31
POOL_WINDOWS = (2, 4, 8, 16)
POOL_GROUP = 128
POOL_OUT_GROUP = 256
N_GROUPS = 4
PER_GROUP = 8
N_EXPERTS = 32
TOP_K = 2
D_EXPERT = 256
N_MOD = 6
EPS = 1e-6

COL_POOL = 2 * D_CONV
COL_GATE_A = COL_POOL + D_POOL
COL_GATE_B = COL_GATE_A + D_MODEL
D_IN = COL_GATE_B + D_MODEL

LANES = 128
SUBLANES = 8
ROW_VREGS = D_MODEL // LANES
HALO = 32
SEQ_TILE = 512
CONV_ROWS = 128
CONV_ROW_STEPS = SEQ_TILE // CONV_ROWS
CONV_LANE_GROUPS = D_CONV // LANES
CONV_ACCS = 2
TAIL_ROWS = 128
GATE_COLS = 256
GATE_COL_BLOCKS = (D_IN - COL_GATE_A) // GATE_COLS
MOE_TILE = 4096
MOE_BLOCK = 512
MOE_BLOCKS = MOE_TILE // MOE_BLOCK
ROUTE_BLOCK = 1024
ROUTE_BLOCKS = MOE_TILE // ROUTE_BLOCK
MOE_CHUNK = 256
MOE_SUB = 128
MOE_TAIL_SUBS = 3
MOE_PAD_ROWS = MOE_SUB
W_SLOTS = 4
CUM_BLOCK = 256
ROUTER_ROWS = 48
GROUP_ROW0 = N_EXPERTS
UNROLL = 32
VMEM_LIMIT = 56 * 1024 * 1024

_F32 = jnp.float32
_BF16 = jnp.bfloat16
assert ROW_VREGS == SUBLANES


def _sigmoid(v):
    return 1.0 / (1.0 + jnp.exp(-v))


def _silu(v):
    return v * _sigmoid(v)


def _rms_norm(v, g):
    return v * lax.rsqrt(jnp.mean(v * v, axis=-1, keepdims=True) + EPS) * g


def _ada_kernel(c_ref, w_ref, b_ref, o_ref):
    c_act = _silu(c_ref[...])
    o_ref[0] = jnp.dot(c_act, w_ref[0], preferred_element_type=_F32,
                       precision=lax.Precision.HIGHEST) + b_ref[0]


def _ada(c, w_ada, b_ada):
    depth, d, n = w_ada.shape
    tn = 1024
    return pl.pallas_call(
        _ada_kernel,
        grid=(depth, n // tn),
        in_specs=[
            pl.BlockSpec(c.shape, lambda l, j: (0, 0)),
            pl.BlockSpec((1, d, tn), lambda l, j: (l, 0, j)),
            pl.BlockSpec((1, 1, tn), lambda l, j: (l, 0, j)),
        ],
        out_specs=pl.BlockSpec((1, c.shape[0], tn), lambda l, j: (l, 0, j)),
        out_shape=jax.ShapeDtypeStruct((depth, c.shape[0], n), _F32),
        name="ada_mod",
    )(c, w_ada, b_ada.reshape(depth, 1, n))


def _mixer_kernel(x_ref, sh_ref, sc_ref, gt_ref, ng_ref, win_ref, cw_ref, cb_ref, lng_ref, lnb_ref,
                  wco_ref, bco_ref, pw_ref, ps_ref, wout_ref, o_ref,
                  gates_ref, e_ref, s_ref, conv_ref, pe_ref, p2_ref, p4_ref, p8_ref):
    ts = SEQ_TILE
    j = pl.program_id(1)

    @pl.when(j == 0)
    def _():
        e_ref[:, 0:HALO, :] = jnp.zeros((CONV_LANE_GROUPS, HALO, LANES), _F32)
        pe_ref[0:HALO, :] = jnp.zeros((HALO, D_POOL), _F32)

    @pl.when(j > 0)
    def _():
        e_ref[:, 0:HALO, :] = e_ref[:, ts:ts + HALO, :]
        pe_ref[0:HALO, :] = pe_ref[ts:ts + HALO, :]

    xt = x_ref[0]
    h = _rms_norm(xt, ng_ref[...]) * (1.0 + sc_ref[0]) + sh_ref[0]
    hb = h.astype(_BF16)

    a = jnp.dot(hb, win_ref[:, 0:D_CONV], preferred_element_type=_F32)
    g = jnp.dot(hb, win_ref[:, D_CONV:2 * D_CONV], preferred_element_type=_F32)
    glu = a * _sigmoid(g)
    for c in range(CONV_LANE_GROUPS):
        e_ref[c, HALO:HALO + ts, :] = glu[:, c * LANES:(c + 1) * LANES]
        for sft in range(1, SUBLANES):
            s_ref[sft - 1, c] = e_ref[c, sft:sft + ts + HALO - SUBLANES, :]

    def conv_block(c, r0, zero_row):
        accs = [jnp.broadcast_to(cb_ref[c] + zero_row, (CONV_ROWS, LANES))] + [None] * (CONV_ACCS - 1)
        for k in range(CONV_K):
            q, sft = divmod(HALO - (CONV_K - 1) + k, SUBLANES)
            lo = r0 + SUBLANES * q
            tap = e_ref[c, lo:lo + CONV_ROWS, :] if sft == 0 else s_ref[sft - 1, c, lo:lo + CONV_ROWS, :]
            prod = tap * cw_ref[k, c]
            a_i = k % CONV_ACCS
            accs[a_i] = prod if accs[a_i] is None else accs[a_i] + prod
        while len(accs) > 1:
            accs = [accs[p] + accs[p + 1] for p in range(0, len(accs), 2)]
        conv_ref[c, r0:r0 + CONV_ROWS, :] = accs[0]

    n_conv = CONV_LANE_GROUPS * CONV_ROW_STEPS
    convs_per_gate = n_conv // GATE_COL_BLOCKS
    for it in range(n_conv):
        if it % convs_per_gate == 0:
            n = it // convs_per_gate
            col = COL_GATE_A + n * GATE_COLS
            gate = _sigmoid(jnp.dot(hb, win_ref[:, col:col + GATE_COLS], preferred_element_type=_F32))
            gates_ref[n] = gate
            zero_row = jnp.where(gate[ts - 1:ts, GATE_COLS - LANES:] > 2.0, 1.0, 0.0)
        conv_block(it // CONV_ROW_STEPS, (it % CONV_ROW_STEPS) * CONV_ROWS, zero_row)

    u = jnp.dot(hb, win_ref[:, COL_POOL:COL_GATE_A], preferred_element_type=_F32)
    pe_ref[HALO:HALO + ts, :] = u
    n_ext = ts + HALO
    p2_ref[8:n_ext, :] = pe_ref[8:n_ext, :] + pe_ref[7:n_ext - 1, :]
    p4_ref[16:n_ext, :] = p2_ref[16:n_ext, :] + p2_ref[14:n_ext - 2, :]
    p8_ref[24:n_ext, :] = p4_ref[24:n_ext, :] + p4_ref[20:n_ext - 4, :]

    half = GATE_COL_BLOCKS // 2
    for rb in range(ts // TAIL_ROWS):
        r0 = rb * TAIL_ROWS
        rows = slice(r0, r0 + TAIL_ROWS)
        ext = slice(HALO + r0, HALO + r0 + TAIL_ROWS)
        v = jnp.concatenate([conv_ref[c, rows, :] for c in range(CONV_LANE_GROUPS)], axis=1)
        mu = jnp.mean(v, axis=-1, keepdims=True)
        vc = v - mu
        var = jnp.mean(vc * vc, axis=-1, keepdims=True)
        v = _silu(vc * lax.rsqrt(var + EPS) * lng_ref[...] + lnb_ref[...])
        y_a = jnp.dot(v.astype(_BF16), wco_ref[...], preferred_element_type=_F32) + bco_ref[...]

        pos = j * ts + r0 + lax.broadcasted_iota(jnp.int32, (TAIL_ROWS, 1), 0) + 1
        y_b_parts = []
        for gi, w in enumerate(POOL_WINDOWS):
            cs = slice(gi * POOL_GROUP, (gi + 1) * POOL_GROUP)
            if w == 2:
                win_sum = p2_ref[ext, cs]
            elif w == 4:
                win_sum = p4_ref[ext, cs]
            elif w == 8:
                win_sum = p8_ref[ext, cs]
            else:
                win_sum = p8_ref[ext, cs] + p8_ref[HALO + r0 - 8:HALO + r0 - 8 + TAIL_ROWS, cs]
            count = jnp.minimum(pos, w).astype(_F32)
            pooled = win_sum / count - pe_ref[ext, cs]
            y_b_parts.append(jnp.dot(pooled.astype(_BF16), pw_ref[gi], preferred_element_type=_F32))
        y_b = jnp.concatenate(y_b_parts, axis=-1) * ps_ref[...]

        gate_a = jnp.concatenate([gates_ref[n, rows, :] for n in range(half)], axis=1)
        gate_b = jnp.concatenate([gates_ref[n, rows, :] for n in range(half, GATE_COL_BLOCKS)], axis=1)
        mixed = gate_a * y_a + gate_b * y_b
        out = jnp.dot(mixed.astype(_BF16), wout_ref[...], preferred_element_type=_F32)
        o_ref[0, rows, :] = x_ref[0, rows, :] + gt_ref[0] * out


def _mixer(x, sh, sc, gt, layer, norm_g, w_in, conv_w, conv_b, ln_g, ln_b, w_co, b_co, pool_w, pool_scale,
           w_out):
    b, s, d = x.shape
    ts = SEQ_TILE
    const2 = lambda bi, j: (0, 0)
    const3 = lambda bi, j: (0, 0, 0)
    row = lambda a: a.reshape(1, -1)

    def layer_spec(a):
        return pl.BlockSpec((None,) + a.shape[1:], lambda bi, j: (layer,) + (0,) * (a.ndim - 1))
    mod_spec = pl.BlockSpec((1, 1, d), lambda bi, j: (bi, 0, 0))
    conv_w = conv_w.reshape(CONV_K, CONV_LANE_GROUPS, 1, LANES)
    conv_b = conv_b.reshape(CONV_LANE_GROUPS, 1, LANES)
    return pl.pallas_call(
        _mixer_kernel,
        grid=(b, s // ts),
        in_specs=[
            pl.BlockSpec((1, ts, d), lambda bi, j: (bi, j, 0)),
            mod_spec, mod_spec, mod_spec,
            pl.BlockSpec((1, d), const2),
            layer_spec(w_in),
            pl.BlockSpec(conv_w.shape, lambda bi, j: (0, 0, 0, 0)),
            pl.BlockSpec(conv_b.shape, const3),
            pl.BlockSpec((1, D_CONV), const2),
            pl.BlockSpec((1, D_CONV), const2),
            layer_spec(w_co),
            pl.BlockSpec((1, d), const2),
            layer_spec(pool_w),
            pl.BlockSpec((1, d), const2),
            layer_spec(w_out),
        ],
        out_specs=pl.BlockSpec((1, ts, d), lambda bi, j: (bi, j, 0)),
        out_shape=jax.ShapeDtypeStruct(x.shape, _F32),
        scratch_shapes=[
            pltpu.VMEM((GATE_COL_BLOCKS, ts, GATE_COLS), _F32),
            pltpu.VMEM((CONV_LANE_GROUPS, ts + HALO, LANES), _F32),
            pltpu.VMEM((SUBLANES - 1, CONV_LANE_GROUPS, ts + HALO - SUBLANES, LANES), _F32),
            pltpu.VMEM((CONV_LANE_GROUPS, ts, LANES), _F32),
            pltpu.VMEM((ts + HALO, D_POOL), _F32),
            pltpu.VMEM((ts + HALO, D_POOL), _F32),
            pltpu.VMEM((ts + HALO, D_POOL), _F32),
            pltpu.VMEM((ts + HALO, D_POOL), _F32),
        ],
        compiler_params=pltpu.CompilerParams(
            dimension_semantics=("arbitrary", "arbitrary"), vmem_limit_bytes=VMEM_LIMIT),
        name="mixer",
    )(x, sh, sc, gt, row(norm_g), w_in, conv_w, conv_b, row(ln_g), row(ln_b), w_co, row(b_co),
      pool_w, row(pool_scale), w_out)


def _route_kernel(x_ref, sh_ref, sc_ref, ng_ref, wrt_ref, brt_ref, pos_ref, wgt_ref, seg_ref,
                  id_scr, rank_scr, wgt_scr, run_ref):
    tb = ROUTE_BLOCK
    j = pl.program_id(1)

    @pl.when(j == 0)
    def _():
        run_ref[...] = jnp.zeros_like(run_ref)

    h = _rms_norm(x_ref[...], ng_ref[...]) * (1.0 + sc_ref[0]) + sh_ref[0]
    nt = (((1,), (1,)), ((), ()))
    h_hi = h.astype(_BF16)
    h_lo = (h - h_hi.astype(_F32)).astype(_BF16)
    w = wrt_ref[...]
    w_hi = w.astype(_BF16)
    w_lo = (w - w_hi.astype(_F32)).astype(_BF16)
    part = lax.dot_general(jnp.concatenate([w_hi, w_lo], axis=0), h_hi, nt, preferred_element_type=_F32)
    logits = (part[0:ROUTER_ROWS] + part[ROUTER_ROWS:]
              + lax.dot_general(w_hi, h_lo, nt, preferred_element_type=_F32)
              + brt_ref[...])
    neg = jnp.float32(-jnp.inf)
    big = jnp.int32(N_EXPERTS)

    grow = lax.broadcasted_iota(jnp.int32, (SUBLANES, tb), 0)
    glog = jnp.where(grow < N_GROUPS, logits[GROUP_ROW0:GROUP_ROW0 + SUBLANES, :], neg)
    gmax = jnp.max(glog, axis=0, keepdims=True)
    p_group = 1.0 / jnp.sum(jnp.exp(glog - gmax), axis=0, keepdims=True)
    g_idx = jnp.min(jnp.where(glog == gmax, grow, big), axis=0, keepdims=True)

    erow = lax.broadcasted_iota(jnp.int32, (N_EXPERTS, tb), 0)
    el = jnp.where((erow // PER_GROUP) == g_idx, logits[0:N_EXPERTS, :], neg)
    m1 = jnp.max(el, axis=0, keepdims=True)
    i1 = jnp.min(jnp.where(el == m1, erow, big), axis=0, keepdims=True)
    el2 = jnp.where(erow == i1, neg, el)
    m2 = jnp.max(el2, axis=0, keepdims=True)
    i2 = jnp.min(jnp.where(el2 == m2, erow, big), axis=0, keepdims=True)
    r = jnp.exp(m2 - m1)
    w1 = p_group / (1.0 + r)
    w2 = w1 * r
    sel1 = erow == i1
    sel2 = erow == i2

    used = jnp.where(sel1 | sel2, 1.0, 0.0)
    tri_r = lax.broadcasted_iota(jnp.int32, (CUM_BLOCK, CUM_BLOCK), 0)
    tri_c = lax.broadcasted_iota(jnp.int32, (CUM_BLOCK, CUM_BLOCK), 1)
    upper = jnp.where(tri_r < tri_c, 1.0, 0.0).astype(_BF16)
    run = run_ref[:, 0:1]
    ranks = []
    for blk in range(tb // CUM_BLOCK):
        ub = used[:, blk * CUM_BLOCK:(blk + 1) * CUM_BLOCK]
        ranks.append(jnp.dot(ub.astype(_BF16), upper, preferred_element_type=_F32) + run)
        run = run + jnp.sum(ub, axis=1, keepdims=True)
    rank = jnp.concatenate(ranks, axis=1)
    run_ref[...] = jnp.broadcast_to(run, run_ref.shape)
    id_scr[j, 0:1, :] = i1
    id_scr[j, 1:2, :] = i2
    rank_scr[j, 0:1, :] = jnp.sum(jnp.where(sel1, rank, 0.0), axis=0, keepdims=True)
    rank_scr[j, 1:2, :] = jnp.sum(jnp.where(sel2, rank, 0.0), axis=0, keepdims=True)
    wgt_scr[j, 0:1, :] = w1
    wgt_scr[j, 1:2, :] = w2

    @pl.when(j == ROUTE_BLOCKS - 1)
    def _():
        cnt = jnp.concatenate([run_ref[...], jnp.zeros((LANES - N_EXPERTS, LANES), _F32)], axis=0)
        low_r = lax.broadcasted_iota(jnp.int32, (LANES, LANES), 0)
        low_c = lax.broadcasted_iota(jnp.int32, (LANES, LANES), 1)
        lower = jnp.where(low_c < low_r, 1.0, 0.0)
        off = jnp.dot(lower, cnt, preferred_element_type=_F32, precision=lax.Precision.HIGHEST)
        seg_ref[0, 0:N_EXPERTS, :] = off[0:N_EXPERTS, :].astype(jnp.int32)
        seg_ref[0, N_EXPERTS:2 * N_EXPERTS, :] = cnt[0:N_EXPERTS, :].astype(jnp.int32)
        off_col = off[0:N_EXPERTS, 0:1]
        for blk in range(ROUTE_BLOCKS):
            ls = slice(blk * tb, (blk + 1) * tb)
            for k in range(TOP_K):
                start = jnp.sum(jnp.where(erow == id_scr[blk, k:k + 1, :], off_col, 0.0),
                                axis=0, keepdims=True)
                slot = start + rank_scr[blk, k:k + 1, :]
                pos_ref[0, k:k + 1, ls] = (slot * ROW_VREGS).astype(jnp.int32)
                wgt_ref[0, k:k + 1, ls] = wgt_scr[blk, k:k + 1, :]


def _route(xf, sh, sc, norm_g, w_rt, b_rt, per_b):
    t, d = xf.shape
    tm, tb, nb = MOE_TILE, ROUTE_BLOCK, ROUTE_BLOCKS
    n_tiles = t // tm
    const2 = lambda i, j: (0, 0)
    mod_spec = pl.BlockSpec((1, 1, d), lambda i, j: (i // per_b, 0, 0))
    out3 = lambda i, j: (i, 0, 0)
    return pl.pallas_call(
        _route_kernel,
        grid=(n_tiles, nb),
        in_specs=[
            pl.BlockSpec((tb, d), lambda i, j: (i * nb + j, 0)),
            mod_spec, mod_spec,
            pl.BlockSpec((1, d), const2),
            pl.BlockSpec((ROUTER_ROWS, d), const2),
            pl.BlockSpec((ROUTER_ROWS, 1), const2),
        ],
        out_specs=[
            pl.BlockSpec((1, TOP_K, tm), out3),
            pl.BlockSpec((1, TOP_K, tm), out3),
            pl.BlockSpec((1, 2 * N_EXPERTS, LANES), out3),
        ],
        out_shape=[
            jax.ShapeDtypeStruct((n_tiles, TOP_K, tm), jnp.int32),
            jax.ShapeDtypeStruct((n_tiles, TOP_K, tm), _F32),
            jax.ShapeDtypeStruct((n_tiles, 2 * N_EXPERTS, LANES), jnp.int32),
        ],
        scratch_shapes=[
            pltpu.VMEM((nb, TOP_K, tb), jnp.int32),
            pltpu.VMEM((nb, TOP_K, tb), _F32),
            pltpu.VMEM((nb, TOP_K, tb), _F32),
            pltpu.VMEM((N_EXPERTS, LANES), _F32),
        ],
        compiler_params=pltpu.CompilerParams(
            dimension_semantics=("arbitrary", "arbitrary"), vmem_limit_bytes=VMEM_LIMIT),
        name="route",
    )(xf, sh, sc, norm_g.reshape(1, d), w_rt, b_rt)


def _row_ds(offset):
    return pl.ds(pl.multiple_of(offset, ROW_VREGS), ROW_VREGS)


def _moe_kernel(pos_s, wgt_s, seg_s,
                x_ref, sh_ref, sc_ref, gt_ref, ng_ref, fg_ref, wg_hbm, wu_hbm, wd_hbm, o_ref,
                tm_ref, srt_ref, wgu_buf, wd_buf, sem, *, layer, final_norm):
    tm, tb, nb, ch = MOE_TILE, MOE_BLOCK, MOE_BLOCKS, MOE_CHUNK
    i = pl.program_id(0)
    j = pl.program_id(1)
    base = i * (TOP_K * tm) + (j % nb) * tb

    def weight_copies(e, slot):
        return (pltpu.make_async_copy(wg_hbm.at[layer, e], wgu_buf.at[slot, :, 0:D_EXPERT], sem.at[0, slot]),
                pltpu.make_async_copy(wu_hbm.at[layer, e], wgu_buf.at[slot, :, D_EXPERT:], sem.at[1, slot]),
                pltpu.make_async_copy(wd_hbm.at[layer, e], wd_buf.at[slot], sem.at[2, slot]))

    @pl.when(j == 0)
    def _():
        for e0 in range(W_SLOTS - 1):
            for cp in weight_copies(e0, e0):
                cp.start()
        srt_ref[TOP_K * tm * ROW_VREGS:, :] = jnp.zeros((MOE_PAD_ROWS * ROW_VREGS, LANES), _F32)

    @pl.when(j < nb)
    def _():
        h = _rms_norm(x_ref[...], ng_ref[...]) * (1.0 + sc_ref[0]) + sh_ref[0]
        for s in range(ROW_VREGS):
            tm_ref[pl.ds(s, tb, stride=ROW_VREGS), :] = h[:, s * LANES:(s + 1) * LANES]

        def dispatch(it, carry):
            t0 = it * UNROLL
            rows = [tm_ref[_row_ds((t0 + u) * ROW_VREGS), :] for u in range(UNROLL)]
            for u in range(UNROLL):
                for k in range(TOP_K):
                    srt_ref[_row_ds(pos_s[base + k * tm + t0 + u]), :] = rows[u]
            return carry

        lax.fori_loop(0, tb // UNROLL, dispatch, 0)

    @pl.when(j == nb)
    def _():
        def expert(e, carry):
            slot = e % W_SLOTS
            for cp in weight_copies(e, slot):
                cp.wait()

            @pl.when(e + W_SLOTS - 1 < N_EXPERTS)
            def _():
                for cp in weight_copies(e + W_SLOTS - 1, (e + W_SLOTS - 1) % W_SLOTS):
                    cp.start()

            off = seg_s[i * (2 * N_EXPERTS) + e]
            cnt = seg_s[i * (2 * N_EXPERTS) + N_EXPERTS + e]

            def run_rows(first_row, n_sub):
                subs = range(n_sub)
                row0 = [first_row + sub * MOE_SUB for sub in subs]
                r0 = [(off + r) * ROW_VREGS for r in row0]
                xs = [jnp.concatenate([srt_ref[pl.ds(r0[sub] + s, MOE_SUB, stride=ROW_VREGS), :]
                                       for s in range(ROW_VREGS)], axis=1) for sub in subs]
                ys = []
                hgus = [jnp.dot(xs[sub].astype(_BF16), wgu_buf[slot], preferred_element_type=_F32)
                        for sub in subs]
                for sub in subs:
                    hgu = hgus[sub]
                    act = _silu(hgu[:, :D_EXPERT]) * hgu[:, D_EXPERT:]
                    y = jnp.dot(act.astype(_BF16), wd_buf[slot], preferred_element_type=_F32)
                    valid = lax.broadcasted_iota(jnp.int32, (MOE_SUB, 1), 0) < (cnt - row0[sub])
                    ys.append(jnp.where(valid, y, xs[sub]))
                for sub in subs:
                    for s in range(ROW_VREGS):
                        srt_ref[pl.ds(r0[sub] + s, MOE_SUB, stride=ROW_VREGS), :] = (
                            ys[sub][:, s * LANES:(s + 1) * LANES])

            def chunk(c, inner):
                run_rows(c * ch, ch // MOE_SUB)
                return inner

            n_whole = jnp.maximum(cnt - MOE_TAIL_SUBS * MOE_SUB + ch - 1, 0) // ch
            lax.fori_loop(0, n_whole, chunk, 0)
            rest = cnt - n_whole * ch
            for n_sub in range(1, MOE_TAIL_SUBS + 1):
                @pl.when((rest > (n_sub - 1) * MOE_SUB) & (rest <= n_sub * MOE_SUB))
                def _(n_sub=n_sub):
                    run_rows(n_whole * ch, n_sub)
            return carry

        lax.fori_loop(0, N_EXPERTS, expert, 0)

    @pl.when(j >= nb)
    def _():
        def combine(it, carry):
            t0 = it * UNROLL
            outs = []
            for u in range(UNROLL):
                t = base + t0 + u
                y0 = srt_ref[_row_ds(pos_s[t]), :]
                y1 = srt_ref[_row_ds(pos_s[t + tm]), :]
                outs.append(wgt_s[t] * y0 + wgt_s[t + tm] * y1)
            for u in range(UNROLL):
                tm_ref[_row_ds((t0 + u) * ROW_VREGS), :] = outs[u]
            return carry

        lax.fori_loop(0, tb // UNROLL, combine, 0)

        moe = jnp.concatenate(
            [tm_ref[pl.ds(s, tb, stride=ROW_VREGS), :] for s in range(ROW_VREGS)], axis=1)
        y = x_ref[...] + gt_ref[0] * moe
        if final_norm:
            y = _rms_norm(y, fg_ref[...])
        o_ref[...] = y


def _moe(x, sh, sc, gt, layer, norm_g, w_rt, b_rt, w_g, w_u, w_d, final_g, final_norm):
    b, s, d = x.shape
    t = b * s
    tm, tb, nb = MOE_TILE, MOE_BLOCK, MOE_BLOCKS
    per_b = s // tm
    n_tiles = t // tm
    xf = x.reshape(t, d)
    pos, wgt, seg = _route(xf, sh, sc, norm_g, w_rt, b_rt, per_b)
    const2 = lambda i, j, *_: (0, 0)
    mod_spec = pl.BlockSpec((1, 1, d), lambda i, j, *_: (i // per_b, 0, 0))
    out = pl.pallas_call(
        functools.partial(_moe_kernel, layer=layer, final_norm=final_norm),
        grid_spec=pltpu.PrefetchScalarGridSpec(
            num_scalar_prefetch=3,
            grid=(n_tiles, 2 * nb),
            in_specs=[
                pl.BlockSpec((tb, d), lambda i, j, *_: (i * nb + j % nb, 0)),
                mod_spec, mod_spec, mod_spec,
                pl.BlockSpec((1, d), const2),
                pl.BlockSpec((1, d), const2),
                pl.BlockSpec(memory_space=pl.ANY),
                pl.BlockSpec(memory_space=pl.ANY),
                pl.BlockSpec(memory_space=pl.ANY),
            ],
            out_specs=pl.BlockSpec((tb, d), lambda i, j, *_: (i * nb + jnp.maximum(j - nb, 0), 0)),
            scratch_shapes=[
                pltpu.VMEM((tb * ROW_VREGS, LANES), _F32),
                pltpu.VMEM(((TOP_K * tm + MOE_PAD_ROWS) * ROW_VREGS, LANES), _F32),
                pltpu.VMEM((W_SLOTS, d, 2 * D_EXPERT), _BF16),
                pltpu.VMEM((W_SLOTS, D_EXPERT, d), _BF16),
                pltpu.SemaphoreType.DMA((3, W_SLOTS)),
            ],
        ),
        out_shape=jax.ShapeDtypeStruct((t, d), _F32),
        compiler_params=pltpu.CompilerParams(
            dimension_semantics=("arbitrary", "arbitrary"), vmem_limit_bytes=VMEM_LIMIT),
        name="moe",
    )(pos.reshape(-1), wgt.reshape(-1), seg[:, :, 0].reshape(-1),
      xf, sh, sc, gt, norm_g.reshape(1, d), final_g.reshape(1, d), w_g, w_u, w_d)
    return out.reshape(b, s, d)


def _router_params(w_rg, b_rg, w_re, b_re):
    d = w_rg.shape[0]
    pad = ROUTER_ROWS - N_EXPERTS - N_GROUPS
    w = jnp.concatenate([w_re.T, w_rg.T, jnp.zeros((pad, d), _F32)], axis=0)
    b = jnp.concatenate([b_re, b_rg, jnp.zeros((pad,), _F32)]).reshape(ROUTER_ROWS, 1)
    return w, b


def kernel(x, c, mixer_norm_g, w_ada, b_ada, w_in, conv_w, conv_b, conv_ln_g, conv_ln_b, w_conv_out,
           b_conv_out, pool_w, pool_scale, w_out, ffn_norm_g, w_router_group, b_router_group,
           w_router_expert, b_router_expert, w_expert_gate, w_expert_up, w_expert_down, final_norm_g):
    depth = w_ada.shape[0]
    bsz = x.shape[0]
    mod = _ada(c, w_ada, b_ada)
    mod = mod.reshape(depth, bsz, N_MOD, 1, D_MODEL)
    w_in_b, w_co_b, pool_w_b, w_out_b = (a.astype(_BF16) for a in (w_in, w_conv_out, pool_w, w_out))
    w_eg_b, w_eu_b, w_ed_b = (a.astype(_BF16) for a in (w_expert_gate, w_expert_up, w_expert_down))
    for l in range(depth):
        sh1, sc1, g1, sh2, sc2, g2 = (mod[l, :, i] for i in range(N_MOD))
        x = _mixer(x, sh1, sc1, g1, l, mixer_norm_g[l], w_in_b, conv_w[l], conv_b[l],
                   conv_ln_g[l], conv_ln_b[l], w_co_b, b_conv_out[l], pool_w_b, pool_scale[l], w_out_b)
        w_rt, b_rt = _router_params(w_router_group[l], b_router_group[l],
                                    w_router_expert[l], b_router_expert[l])
        x = _moe(x, sh2, sc2, g2, l, ffn_norm_g[l], w_rt, b_rt, w_eg_b, w_eu_b, w_ed_b,
                 final_norm_g, final_norm=(l == depth - 1))
    return x
```

```python
import functools

import jax
import jax.numpy as jnp
from jax import lax
from jax.experimental import pallas as pl
from jax.experimental.pallas import tpu as pltpu

D_MODEL = 1024
D_CONV = 512
D_POOL = 512
CONV_K = 31
POOL_WINDOWS = (2, 4, 8, 16)
POOL_GROUP = 128
POOL_OUT_GROUP = 256
N_GROUPS = 4
PER_GROUP = 8
N_EXPERTS = 32
TOP_K = 2
D_EXPERT = 256
N_MOD = 6
EPS = 1e-6

COL_POOL = 2 * D_CONV
COL_GATE_A = COL_POOL + D_POOL
COL_GATE_B = COL_GATE_A + D_MODEL
D_IN = COL_GATE_B + D_MODEL

LANES = 128
SUBLANES = 8
ROW_VREGS = D_MODEL // LANES
HALO = 32
SEQ_TILE = 512
CONV_ROWS = 128
CONV_ROW_STEPS = SEQ_TILE // CONV_ROWS
CONV_LANE_GROUPS = D_CONV // LANES
CONV_ACCS = 2
TAIL_ROWS = 128
GATE_COLS = 256
GATE_COL_BLOCKS = (D_IN - COL_GATE_A) // GATE_COLS
MOE_TILE = 4096
MOE_BLOCK = 512
MOE_BLOCKS = MOE_TILE // MOE_BLOCK
ROUTE_BLOCK = 2048
ROUTE_BLOCKS = MOE_TILE // ROUTE_BLOCK
MOE_CHUNK = 256
MOE_SUB = 128
MOE_TAIL_SUBS = 3
MOE_PAD_ROWS = MOE_SUB
W_SLOTS = 4
CUM_BLOCK = 256
ROUTER_ROWS = 48
GROUP_ROW0 = N_EXPERTS
UNROLL = 32
VMEM_LIMIT = 56 * 1024 * 1024

_F32 = jnp.float32
_BF16 = jnp.bfloat16
assert ROW_VREGS == SUBLANES


def _sigmoid(v):
    return 1.0 / (1.0 + jnp.exp(-v))


def _silu(v):
    return v * _sigmoid(v)


def _rms_norm(v, g):
    return v * lax.rsqrt(jnp.mean(v * v, axis=-1, keepdims=True) + EPS) * g


def _ada_kernel(c_ref, w_ref, b_ref, o_ref):
    c_act = _silu(c_ref[...])
    o_ref[0] = jnp.dot(c_act, w_ref[0], preferred_element_type=_F32,
                       precision=lax.Precision.HIGHEST) + b_ref[0]


def _ada(c, w_ada, b_ada):
    depth, d, n = w_ada.shape
    tn = 1024
    return pl.pallas_call(
        _ada_kernel,
        grid=(depth, n // tn),
        in_specs=[
            pl.BlockSpec(c.shape, lambda l, j: (0, 0)),
            pl.BlockSpec((1, d, tn), lambda l, j: (l, 0, j)),
            pl.BlockSpec((1, 1, tn), lambda l, j: (l, 0, j)),
        ],
        out_specs=pl.BlockSpec((1, c.shape[0], tn), lambda l, j: (l, 0, j)),
        out_shape=jax.ShapeDtypeStruct((depth, c.shape[0], n), _F32),
        name="ada_mod",
    )(c, w_ada, b_ada.reshape(depth, 1, n))


def _mixer_kernel(x_ref, sh_ref, sc_ref, gt_ref, ng_ref, win_ref, cw_ref, cb_ref, lng_ref, lnb_ref,
                  wco_ref, bco_ref, pw_ref, ps_ref, wout_ref, o_ref,
                  gates_ref, e_ref, s_ref, conv_ref, pe_ref, p2_ref, p4_ref, p8_ref):
    ts = SEQ_TILE
    j = pl.program_id(1)

    @pl.when(j == 0)
    def _():
        e_ref[:, 0:HALO, :] = jnp.zeros((CONV_LANE_GROUPS, HALO, LANES), _F32)
        pe_ref[0:HALO, :] = jnp.zeros((HALO, D_POOL), _F32)

    @pl.when(j > 0)
    def _():
        e_ref[:, 0:HALO, :] = e_ref[:, ts:ts + HALO, :]
        pe_ref[0:HALO, :] = pe_ref[ts:ts + HALO, :]

    xt = x_ref[0]
    h = _rms_norm(xt, ng_ref[...]) * (1.0 + sc_ref[0]) + sh_ref[0]
    hb = h.astype(_BF16)

    a = jnp.dot(hb, win_ref[:, 0:D_CONV], preferred_element_type=_F32)
    g = jnp.dot(hb, win_ref[:, D_CONV:2 * D_CONV], preferred_element_type=_F32)
    glu = a * _sigmoid(g)
    for c in range(CONV_LANE_GROUPS):
        e_ref[c, HALO:HALO + ts, :] = glu[:, c * LANES:(c + 1) * LANES]
        for sft in range(1, SUBLANES):
            s_ref[sft - 1, c] = e_ref[c, sft:sft + ts + HALO - SUBLANES, :]

    def conv_block(c, r0, zero_row):
        accs = [jnp.broadcast_to(cb_ref[c] + zero_row, (CONV_ROWS, LANES))] + [None] * (CONV_ACCS - 1)
        for k in range(CONV_K):
            q, sft = divmod(HALO - (CONV_K - 1) + k, SUBLANES)
            lo = r0 + SUBLANES * q
            tap = e_ref[c, lo:lo + CONV_ROWS, :] if sft == 0 else s_ref[sft - 1, c, lo:lo + CONV_ROWS, :]
            prod = tap * cw_ref[k, c]
            a_i = k % CONV_ACCS
            accs[a_i] = prod if accs[a_i] is None else accs[a_i] + prod
        while len(accs) > 1:
            accs = [accs[p] + accs[p + 1] for p in range(0, len(accs), 2)]
        conv_ref[c, r0:r0 + CONV_ROWS, :] = accs[0]

    n_conv = CONV_LANE_GROUPS * CONV_ROW_STEPS
    convs_per_gate = n_conv // GATE_COL_BLOCKS
    zero_row = next_zero_row = jnp.zeros((1, LANES), _F32)
    for it in range(n_conv):
        if it % convs_per_gate == 0:
            n = it // convs_per_gate
            col = COL_GATE_A + n * GATE_COLS
            gate = _sigmoid(jnp.dot(hb, win_ref[:, col:col + GATE_COLS], preferred_element_type=_F32))
            gates_ref[n] = gate
            zero_row = next_zero_row
            next_zero_row = jnp.where(gate[ts - 1:ts, GATE_COLS - LANES:] > 2.0, 1.0, 0.0)
        conv_block(it // CONV_ROW_STEPS, (it % CONV_ROW_STEPS) * CONV_ROWS, zero_row)

    u = jnp.dot(hb, win_ref[:, COL_POOL:COL_GATE_A], preferred_element_type=_F32)
    pe_ref[HALO:HALO + ts, :] = u
    n_ext = ts + HALO
    p2_ref[8:n_ext, :] = pe_ref[8:n_ext, :] + pe_ref[7:n_ext - 1, :]
    p4_ref[16:n_ext, :] = p2_ref[16:n_ext, :] + p2_ref[14:n_ext - 2, :]
    p8_ref[24:n_ext, :] = p4_ref[24:n_ext, :] + p4_ref[20:n_ext - 4, :]

    half = GATE_COL_BLOCKS // 2
    for rb in range(ts // TAIL_ROWS):
        r0 = rb * TAIL_ROWS
        rows = slice(r0, r0 + TAIL_ROWS)
        ext = slice(HALO + r0, HALO + r0 + TAIL_ROWS)
        v = jnp.concatenate([conv_ref[c, rows, :] for c in range(CONV_LANE_GROUPS)], axis=1)
        mu = jnp.mean(v, axis=-1, keepdims=True)
        vc = v - mu
        var = jnp.mean(vc * vc, axis=-1, keepdims=True)
        v = _silu(vc * lax.rsqrt(var + EPS) * lng_ref[...] + lnb_ref[...])
        y_a = jnp.dot(v.astype(_BF16), wco_ref[...], preferred_element_type=_F32) + bco_ref[...]

        pos = j * ts + r0 + lax.broadcasted_iota(jnp.int32, (TAIL_ROWS, 1), 0) + 1
        y_b_parts = []
        for gi, w in enumerate(POOL_WINDOWS):
            cs = slice(gi * POOL_GROUP, (gi + 1) * POOL_GROUP)
            if w == 2:
                win_sum = p2_ref[ext, cs]
            elif w == 4:
                win_sum = p4_ref[ext, cs]
            elif w == 8:
                win_sum = p8_ref[ext, cs]
            else:
                win_sum = p8_ref[ext, cs] + p8_ref[HALO + r0 - 8:HALO + r0 - 8 + TAIL_ROWS, cs]
            count = jnp.minimum(pos, w).astype(_F32)
            pooled = win_sum / count - pe_ref[ext, cs]
            y_b_parts.append(jnp.dot(pooled.astype(_BF16), pw_ref[gi], preferred_element_type=_F32))
        y_b = jnp.concatenate(y_b_parts, axis=-1) * ps_ref[...]

        gate_a = jnp.concatenate([gates_ref[n, rows, :] for n in range(half)], axis=1)
        gate_b = jnp.concatenate([gates_ref[n, rows, :] for n in range(half, GATE_COL_BLOCKS)], axis=1)
        mixed = gate_a * y_a + gate_b * y_b
        out = jnp.dot(mixed.astype(_BF16), wout_ref[...], preferred_element_type=_F32)
        o_ref[0, rows, :] = x_ref[0, rows, :] + gt_ref[0] * out


def _mixer(x, sh, sc, gt, layer, norm_g, w_in, conv_w, conv_b, ln_g, ln_b, w_co, b_co, pool_w, pool_scale,
           w_out):
    b, s, d = x.shape
    ts = SEQ_TILE
    const2 = lambda bi, j: (0, 0)
    const3 = lambda bi, j: (0, 0, 0)
    row = lambda a: a.reshape(1, -1)

    def layer_spec(a):
        return pl.BlockSpec((None,) + a.shape[1:], lambda bi, j: (layer,) + (0,) * (a.ndim - 1))
    mod_spec = pl.BlockSpec((1, 1, d), lambda bi, j: (bi, 0, 0))
    conv_w = conv_w.reshape(CONV_K, CONV_LANE_GROUPS, 1, LANES)
    conv_b = conv_b.reshape(CONV_LANE_GROUPS, 1, LANES)
    return pl.pallas_call(
        _mixer_kernel,
        grid=(b, s // ts),
        in_specs=[
            pl.BlockSpec((1, ts, d), lambda bi, j: (bi, j, 0)),
            mod_spec, mod_spec, mod_spec,
            pl.BlockSpec((1, d), const2),
            layer_spec(w_in),
            pl.BlockSpec(conv_w.shape, lambda bi, j: (0, 0, 0, 0)),
            pl.BlockSpec(conv_b.shape, const3),
            pl.BlockSpec((1, D_CONV), const2),
            pl.BlockSpec((1, D_CONV), const2),
            layer_spec(w_co),
            pl.BlockSpec((1, d), const2),
            layer_spec(pool_w),
            pl.BlockSpec((1, d), const2),
            layer_spec(w_out),
        ],
        out_specs=pl.BlockSpec((1, ts, d), lambda bi, j: (bi, j, 0)),
        out_shape=jax.ShapeDtypeStruct(x.shape, _F32),
        scratch_shapes=[
            pltpu.VMEM((GATE_COL_BLOCKS, ts, GATE_COLS), _F32),
            pltpu.VMEM((CONV_LANE_GROUPS, ts + HALO, LANES), _F32),
            pltpu.VMEM((SUBLANES - 1, CONV_LANE_GROUPS, ts + HALO - SUBLANES, LANES), _F32),
            pltpu.VMEM((CONV_LANE_GROUPS, ts, LANES), _F32),
            pltpu.VMEM((ts + HALO, D_POOL), _F32),
            pltpu.VMEM((ts + HALO, D_POOL), _F32),
            pltpu.VMEM((ts + HALO, D_POOL), _F32),
            pltpu.VMEM((ts + HALO, D_POOL), _F32),
        ],
        compiler_params=pltpu.CompilerParams(
            dimension_semantics=("arbitrary", "arbitrary"), vmem_limit_bytes=VMEM_LIMIT),
        name="mixer",
    )(x, sh, sc, gt, row(norm_g), w_in, conv_w, conv_b, row(ln_g), row(ln_b), w_co, row(b_co),
      pool_w, row(pool_scale), w_out)


def _route_kernel(x_ref, sh_ref, sc_ref, ng_ref, wrt_ref, brt_ref, pos_ref, wgt_ref, seg_ref,
                  id_scr, rank_scr, wgt_scr, run_ref):
    tb = ROUTE_BLOCK
    j = pl.program_id(1)

    @pl.when(j == 0)
    def _():
        run_ref[...] = jnp.zeros_like(run_ref)

    h = _rms_norm(x_ref[...], ng_ref[...]) * (1.0 + sc_ref[0]) + sh_ref[0]
    nt = (((1,), (1,)), ((), ()))
    h_hi = h.astype(_BF16)
    h_lo = (h - h_hi.astype(_F32)).astype(_BF16)
    w = wrt_ref[...]
    w_hi = w.astype(_BF16)
    w_lo = (w - w_hi.astype(_F32)).astype(_BF16)
    part = lax.dot_general(jnp.concatenate([w_hi, w_lo], axis=0), h_hi, nt, preferred_element_type=_F32)
    logits = (part[0:ROUTER_ROWS] + part[ROUTER_ROWS:]
              + lax.dot_general(w_hi, h_lo, nt, preferred_element_type=_F32)
              + brt_ref[...])
    neg = jnp.float32(-jnp.inf)
    big = jnp.int32(N_EXPERTS)

    grow = lax.broadcasted_iota(jnp.int32, (SUBLANES, tb), 0)
    glog = jnp.where(grow < N_GROUPS, logits[GROUP_ROW0:GROUP_ROW0 + SUBLANES, :], neg)
    gmax = jnp.max(glog, axis=0, keepdims=True)
    p_group = 1.0 / jnp.sum(jnp.exp(glog - gmax), axis=0, keepdims=True)
    g_idx = jnp.min(jnp.where(glog == gmax, grow, big), axis=0, keepdims=True)

    erow = lax.broadcasted_iota(jnp.int32, (N_EXPERTS, tb), 0)
    el = jnp.where((erow // PER_GROUP) == g_idx, logits[0:N_EXPERTS, :], neg)
    m1 = jnp.max(el, axis=0, keepdims=True)
    i1 = jnp.min(jnp.where(el == m1, erow, big), axis=0, keepdims=True)
    el2 = jnp.where(erow == i1, neg, el)
    m2 = jnp.max(el2, axis=0, keepdims=True)
    i2 = jnp.min(jnp.where(el2 == m2, erow, big), axis=0, keepdims=True)
    r = jnp.exp(m2 - m1)
    w1 = p_group / (1.0 + r)
    w2 = w1 * r
    sel1 = erow == i1
    sel2 = erow == i2

    used = jnp.where(sel1 | sel2, 1.0, 0.0)
    tri_r = lax.broadcasted_iota(jnp.int32, (CUM_BLOCK, CUM_BLOCK), 0)
    tri_c = lax.broadcasted_iota(jnp.int32, (CUM_BLOCK, CUM_BLOCK), 1)
    upper = jnp.where(tri_r < tri_c, 1.0, 0.0).astype(_BF16)
    run = run_ref[:, 0:1]
    ranks = []
    for blk in range(tb // CUM_BLOCK):
        ub = used[:, blk * CUM_BLOCK:(blk + 1) * CUM_BLOCK]
        ranks.append(jnp.dot(ub.astype(_BF16), upper, preferred_element_type=_F32) + run)
        run = run + jnp.sum(ub, axis=1, keepdims=True)
    rank = jnp.concatenate(ranks, axis=1)
    run_ref[...] = jnp.broadcast_to(run, run_ref.shape)
    id_scr[j, 0:1, :] = i1
    id_scr[j, 1:2, :] = i2
    rank_scr[j, 0:1, :] = jnp.sum(jnp.where(sel1, rank, 0.0), axis=0, keepdims=True)
    rank_scr[j, 1:2, :] = jnp.sum(jnp.where(sel2, rank, 0.0), axis=0, keepdims=True)
    wgt_scr[j, 0:1, :] = w1
    wgt_scr[j, 1:2, :] = w2

    @pl.when(j == ROUTE_BLOCKS - 1)
    def _():
        cnt = jnp.concatenate([run_ref[...], jnp.zeros((LANES - N_EXPERTS, LANES), _F32)], axis=0)
        low_r = lax.broadcasted_iota(jnp.int32, (LANES, LANES), 0)
        low_c = lax.broadcasted_iota(jnp.int32, (LANES, LANES), 1)
        lower = jnp.where(low_c < low_r, 1.0, 0.0)
        off = jnp.dot(lower, cnt, preferred_element_type=_F32, precision=lax.Precision.HIGHEST)
        seg_ref[0, 0:N_EXPERTS, :] = off[0:N_EXPERTS, :].astype(jnp.int32)
        seg_ref[0, N_EXPERTS:2 * N_EXPERTS, :] = cnt[0:N_EXPERTS, :].astype(jnp.int32)
        off_col = off[0:N_EXPERTS, 0:1]
        for blk in range(ROUTE_BLOCKS):
            ls = slice(blk * tb, (blk + 1) * tb)
            for k in range(TOP_K):
                start = jnp.sum(jnp.where(erow == id_scr[blk, k:k + 1, :], off_col, 0.0),
                                axis=0, keepdims=True)
                slot = start + rank_scr[blk, k:k + 1, :]
                pos_ref[0, k:k + 1, ls] = (slot * ROW_VREGS).astype(jnp.int32)
                wgt_ref[0, k:k + 1, ls] = wgt_scr[blk, k:k + 1, :]


def _route(xf, sh, sc, norm_g, w_rt, b_rt, per_b):
    t, d = xf.shape
    tm, tb, nb = MOE_TILE, ROUTE_BLOCK, ROUTE_BLOCKS
    n_tiles = t // tm
    const2 = lambda i, j: (0, 0)
    mod_spec = pl.BlockSpec((1, 1, d), lambda i, j: (i // per_b, 0, 0))
    out3 = lambda i, j: (i, 0, 0)
    return pl.pallas_call(
        _route_kernel,
        grid=(n_tiles, nb),
        in_specs=[
            pl.BlockSpec((tb, d), lambda i, j: (i * nb + j, 0)),
            mod_spec, mod_spec,
            pl.BlockSpec((1, d), const2),
            pl.BlockSpec((ROUTER_ROWS, d), const2),
            pl.BlockSpec((ROUTER_ROWS, 1), const2),
        ],
        out_specs=[
            pl.BlockSpec((1, TOP_K, tm), out3),
            pl.BlockSpec((1, TOP_K, tm), out3),
            pl.BlockSpec((1, 2 * N_EXPERTS, LANES), out3),
        ],
        out_shape=[
            jax.ShapeDtypeStruct((n_tiles, TOP_K, tm), jnp.int32),
            jax.ShapeDtypeStruct((n_tiles, TOP_K, tm), _F32),
            jax.ShapeDtypeStruct((n_tiles, 2 * N_EXPERTS, LANES), jnp.int32),
        ],
        scratch_shapes=[
            pltpu.VMEM((nb, TOP_K, tb), jnp.int32),
            pltpu.VMEM((nb, TOP_K, tb), _F32),
            pltpu.VMEM((nb, TOP_K, tb), _F32),
            pltpu.VMEM((N_EXPERTS, LANES), _F32),
        ],
        compiler_params=pltpu.CompilerParams(
            dimension_semantics=("arbitrary", "arbitrary"), vmem_limit_bytes=VMEM_LIMIT),
        name="route",
    )(xf, sh, sc, norm_g.reshape(1, d), w_rt, b_rt)


def _row_ds(offset):
    return pl.ds(pl.multiple_of(offset, ROW_VREGS), ROW_VREGS)


def _moe_kernel(pos_s, wgt_s, seg_s,
                x_ref, sh_ref, sc_ref, gt_ref, ng_ref, fg_ref, wg_hbm, wu_hbm, wd_hbm, o_ref,
                tm_ref, srt_ref, wgu_buf, wd_buf, sem, *, layer, final_norm):
    tm, tb, nb, ch = MOE_TILE, MOE_BLOCK, MOE_BLOCKS, MOE_CHUNK
    i = pl.program_id(0)
    j = pl.program_id(1)
    base = i * (TOP_K * tm) + (j % nb) * tb

    def weight_copies(e, slot):
        return (pltpu.make_async_copy(wg_hbm.at[layer, e], wgu_buf.at[slot, :, 0:D_EXPERT], sem.at[0, slot]),
                pltpu.make_async_copy(wu_hbm.at[layer, e], wgu_buf.at[slot, :, D_EXPERT:], sem.at[1, slot]),
                pltpu.make_async_copy(wd_hbm.at[layer, e], wd_buf.at[slot], sem.at[2, slot]))

    @pl.when(j == 0)
    def _():
        for e0 in range(W_SLOTS - 1):
            for cp in weight_copies(e0, e0):
                cp.start()
        srt_ref[TOP_K * tm * ROW_VREGS:, :] = jnp.zeros((MOE_PAD_ROWS * ROW_VREGS, LANES), _F32)

    @pl.when(j < nb)
    def _():
        h = _rms_norm(x_ref[...], ng_ref[...]) * (1.0 + sc_ref[0]) + sh_ref[0]
        for s in range(ROW_VREGS):
            tm_ref[pl.ds(s, tb, stride=ROW_VREGS), :] = h[:, s * LANES:(s + 1) * LANES]

        def dispatch(it, carry):
            t0 = it * UNROLL
            rows = [tm_ref[_row_ds((t0 + u) * ROW_VREGS), :] for u in range(UNROLL)]
            for u in range(UNROLL):
                for k in range(TOP_K):
                    srt_ref[_row_ds(pos_s[base + k * tm + t0 + u]), :] = rows[u]
            return carry

        lax.fori_loop(0, tb // UNROLL, dispatch, 0)

    @pl.when(j == nb)
    def _():
        def expert(e, carry):
            slot = e % W_SLOTS
            for cp in weight_copies(e, slot):
                cp.wait()

            @pl.when(e + W_SLOTS - 1 < N_EXPERTS)
            def _():
                for cp in weight_copies(e + W_SLOTS - 1, (e + W_SLOTS - 1) % W_SLOTS):
                    cp.start()

            off = seg_s[i * (2 * N_EXPERTS) + e]
            cnt = seg_s[i * (2 * N_EXPERTS) + N_EXPERTS + e]

            def run_rows(first_row, n_sub):
                subs = range(n_sub)
                row0 = [first_row + sub * MOE_SUB for sub in subs]
                r0 = [(off + r) * ROW_VREGS for r in row0]
                xs = [jnp.concatenate([srt_ref[pl.ds(r0[sub] + s, MOE_SUB, stride=ROW_VREGS), :]
                                       for s in range(ROW_VREGS)], axis=1) for sub in subs]
                ys = []
                hgus = [jnp.dot(xs[sub].astype(_BF16), wgu_buf[slot], preferred_element_type=_F32)
                        for sub in subs]
                for sub in subs:
                    hgu = hgus[sub]
                    act = _silu(hgu[:, :D_EXPERT]) * hgu[:, D_EXPERT:]
                    y = jnp.dot(act.astype(_BF16), wd_buf[slot], preferred_element_type=_F32)
                    valid = lax.broadcasted_iota(jnp.int32, (MOE_SUB, 1), 0) < (cnt - row0[sub])
                    ys.append(jnp.where(valid, y, xs[sub]))
                for sub in subs:
                    for s in range(ROW_VREGS):
                        srt_ref[pl.ds(r0[sub] + s, MOE_SUB, stride=ROW_VREGS), :] = (
                            ys[sub][:, s * LANES:(s + 1) * LANES])

            def chunk(c, inner):
                run_rows(c * ch, ch // MOE_SUB)
                return inner

            n_whole = jnp.maximum(cnt - MOE_TAIL_SUBS * MOE_SUB + ch - 1, 0) // ch
            lax.fori_loop(0, n_whole, chunk, 0)
            rest = cnt - n_whole * ch
            for n_sub in range(1, MOE_TAIL_SUBS + 1):
                @pl.when((rest > (n_sub - 1) * MOE_SUB) & (rest <= n_sub * MOE_SUB))
                def _(n_sub=n_sub):
                    run_rows(n_whole * ch, n_sub)
            return carry

        lax.fori_loop(0, N_EXPERTS, expert, 0)

    @pl.when(j >= nb)
    def _():
        def combine(it, carry):
            t0 = it * UNROLL
            outs = []
            for u in range(UNROLL):
                t = base + t0 + u
                y0 = srt_ref[_row_ds(pos_s[t]), :]
                y1 = srt_ref[_row_ds(pos_s[t + tm]), :]
                outs.append(wgt_s[t] * y0 + wgt_s[t + tm] * y1)
            for u in range(UNROLL):
                tm_ref[_row_ds((t0 + u) * ROW_VREGS), :] = outs[u]
            return carry

        lax.fori_loop(0, tb // UNROLL, combine, 0)

        moe = jnp.concatenate(
            [tm_ref[pl.ds(s, tb, stride=ROW_VREGS), :] for s in range(ROW_VREGS)], axis=1)
        y = x_ref[...] + gt_ref[0] * moe
        if final_norm:
            y = _rms_norm(y, fg_ref[...])
        o_ref[...] = y


def _moe(x, sh, sc, gt, layer, norm_g, w_rt, b_rt, w_g, w_u, w_d, final_g, final_norm):
    b, s, d = x.shape
    t = b * s
    tm, tb, nb = MOE_TILE, MOE_BLOCK, MOE_BLOCKS
    per_b = s // tm
    n_tiles = t // tm
    xf = x.reshape(t, d)
    pos, wgt, seg = _route(xf, sh, sc, norm_g, w_rt, b_rt, per_b)
    const2 = lambda i, j, *_: (0, 0)
    mod_spec = pl.BlockSpec((1, 1, d), lambda i, j, *_: (i // per_b, 0, 0))
    out = pl.pallas_call(
        functools.partial(_moe_kernel, layer=layer, final_norm=final_norm),
        grid_spec=pltpu.PrefetchScalarGridSpec(
            num_scalar_prefetch=3,
            grid=(n_tiles, 2 * nb),
            in_specs=[
                pl.BlockSpec((tb, d), lambda i, j, *_: (i * nb + j % nb, 0)),
                mod_spec, mod_spec, mod_spec,
                pl.BlockSpec((1, d), const2),
                pl.BlockSpec((1, d), const2),
                pl.BlockSpec(memory_space=pl.ANY),
                pl.BlockSpec(memory_space=pl.ANY),
                pl.BlockSpec(memory_space=pl.ANY),
            ],
            out_specs=pl.BlockSpec((tb, d), lambda i, j, *_: (i * nb + jnp.maximum(j - nb, 0), 0)),
            scratch_shapes=[
                pltpu.VMEM((tb * ROW_VREGS, LANES), _F32),
                pltpu.VMEM(((TOP_K * tm + MOE_PAD_ROWS) * ROW_VREGS, LANES), _F32),
                pltpu.VMEM((W_SLOTS, d, 2 * D_EXPERT), _BF16),
                pltpu.VMEM((W_SLOTS, D_EXPERT, d), _BF16),
                pltpu.SemaphoreType.DMA((3, W_SLOTS)),
            ],
        ),
        out_shape=jax.ShapeDtypeStruct((t, d), _F32),
        compiler_params=pltpu.CompilerParams(
            dimension_semantics=("arbitrary", "arbitrary"), vmem_limit_bytes=VMEM_LIMIT),
        name="moe",
    )(pos.reshape(-1), wgt.reshape(-1), seg[:, :, 0].reshape(-1),
      xf, sh, sc, gt, norm_g.reshape(1, d), final_g.reshape(1, d), w_g, w_u, w_d)
    return out.reshape(b, s, d)


def _router_params(w_rg, b_rg, w_re, b_re):
    d = w_rg.shape[0]
    pad = ROUTER_ROWS - N_EXPERTS - N_GROUPS
    w = jnp.concatenate([w_re.T, w_rg.T, jnp.zeros((pad, d), _F32)], axis=0)
    b = jnp.concatenate([b_re, b_rg, jnp.zeros((pad,), _F32)]).reshape(ROUTER_ROWS, 1)
    return w, b


def kernel(x, c, mixer_norm_g, w_ada, b_ada, w_in, conv_w, conv_b, conv_ln_g, conv_ln_b, w_conv_out,
           b_conv_out, pool_w, pool_scale, w_out, ffn_norm_g, w_router_group, b_router_group,
           w_router_expert, b_router_expert, w_expert_gate, w_expert_up, w_expert_down, final_norm_g):
    depth = w_ada.shape[0]
    bsz = x.shape[0]
    mod = _ada(c, w_ada, b_ada)
    mod = mod.reshape(depth, bsz, N_MOD, 1, D_MODEL)
    w_in_b, w_co_b, pool_w_b, w_out_b = (a.astype(_BF16) for a in (w_in, w_conv_out, pool_w, w_out))
    w_eg_b, w_eu_b, w_ed_b = (a.astype(_BF16) for a in (w_expert_gate, w_expert_up, w_expert_down))
    for l in range(depth):
        sh1, sc1, g1, sh2, sc2, g2 = (mod[l, :, i] for i in range(N_MOD))
        x = _mixer(x, sh1, sc1, g1, l, mixer_norm_g[l], w_in_b, conv_w[l], conv_b[l],
                   conv_ln_g[l], conv_ln_b[l], w_co_b, b_conv_out[l], pool_w_b, pool_scale[l], w_out_b)
        w_rt, b_rt = _router_params(w_router_group[l], b_router_group[l],
                                    w_router_expert[l], b_router_expert[l])
        x = _moe(x, sh2, sc2, g2, l, ffn_norm_g[l], w_rt, b_rt, w_eg_b, w_eu_b, w_ed_b,
                 final_norm_g, final_norm=(l == depth - 1))
    return x
```

```python
import functools

import jax
import jax.numpy as jnp
from jax import lax
from jax.experimental import pallas as pl
from jax.experimental.pallas import tpu as pltpu

D_MODEL = 1024
D_CONV = 512
D_POOL = 512
CONV_K = 31
POOL_WINDOWS = (2, 4, 8, 16)
POOL_GROUP = 128
POOL_OUT_GROUP = 256
N_GROUPS = 4
PER_GROUP = 8
N_EXPERTS = 32
TOP_K = 2
D_EXPERT = 256
N_MOD = 6
EPS = 1e-6

COL_POOL = 2 * D_CONV
COL_GATE_A = COL_POOL + D_POOL
COL_GATE_B = COL_GATE_A + D_MODEL
D_IN = COL_GATE_B + D_MODEL

LANES = 128
SUBLANES = 8
ROW_VREGS = D_MODEL // LANES
HALO = 32
CONV_HALO_VREGS = 32
SEQ_TILE = 512
CONV_ROWS = 128
CONV_ROW_STEPS = SEQ_TILE // CONV_ROWS
CONV_LANE_GROUPS = D_CONV // LANES
CONV_ACCS = 2
TAIL_ROWS = 128
GATE_COLS = 256
GATE_COL_BLOCKS = (D_IN - COL_GATE_A) // GATE_COLS
MOE_TILE = 4096
MOE_BLOCK = 512
MOE_BLOCKS = MOE_TILE // MOE_BLOCK
ROUTE_BLOCK = 2048
ROUTE_BLOCKS = MOE_TILE // ROUTE_BLOCK
MOE_CHUNK = 256
MOE_SUB = 128
MOE_TAIL_SUBS = 3
MOE_PAD_ROWS = MOE_SUB
W_SLOTS = 4
CUM_BLOCK = 256
ROUTER_ROWS = 48
GROUP_ROW0 = N_EXPERTS
UNROLL = 32
VMEM_LIMIT = 56 * 1024 * 1024

_F32 = jnp.float32
_BF16 = jnp.bfloat16
assert ROW_VREGS == SUBLANES


def _sigmoid(v):
    return 1.0 / (1.0 + jnp.exp(-v))


def _silu(v):
    return v * _sigmoid(v)


def _rms_norm(v, g):
    return v * lax.rsqrt(jnp.mean(v * v, axis=-1, keepdims=True) + EPS) * g


def _ada_kernel(c_ref, w_ref, b_ref, o_ref):
    c_act = _silu(c_ref[...])
    o_ref[0] = jnp.dot(c_act, w_ref[0], preferred_element_type=_F32,
                       precision=lax.Precision.HIGHEST) + b_ref[0]


def _ada(c, w_ada, b_ada):
    depth, d, n = w_ada.shape
    tn = 1024
    return pl.pallas_call(
        _ada_kernel,
        grid=(depth, n // tn),
        in_specs=[
            pl.BlockSpec(c.shape, lambda l, j: (0, 0)),
            pl.BlockSpec((1, d, tn), lambda l, j: (l, 0, j)),
            pl.BlockSpec((1, 1, tn), lambda l, j: (l, 0, j)),
        ],
        out_specs=pl.BlockSpec((1, c.shape[0], tn), lambda l, j: (l, 0, j)),
        out_shape=jax.ShapeDtypeStruct((depth, c.shape[0], n), _F32),
        name="ada_mod",
    )(c, w_ada, b_ada.reshape(depth, 1, n))


def _mixer_kernel(x_ref, sh_ref, sc_ref, gt_ref, ng_ref, win_ref, cw_ref, cb_ref, lng_ref, lnb_ref,
                  wco_ref, bco_ref, pw_ref, ps_ref, wout_ref, o_ref,
                  gates_ref, xil_ref, prev_ref, conv_ref, pe_ref, p2_ref, p4_ref, p8_ref):
    ts = SEQ_TILE
    seg = ts // SUBLANES
    halo_rows = CONV_HALO_VREGS * SUBLANES
    j = pl.program_id(1)

    @pl.when(j == 0)
    def _():
        prev_ref[...] = jnp.zeros_like(prev_ref)
        pe_ref[0:HALO, :] = jnp.zeros((HALO, D_POOL), _F32)

    @pl.when(j > 0)
    def _():
        prev_ref[...] = xil_ref[:, seg * SUBLANES:seg * SUBLANES + halo_rows, :]
        pe_ref[0:HALO, :] = pe_ref[ts:ts + HALO, :]

    xt = x_ref[0]
    h = _rms_norm(xt, ng_ref[...]) * (1.0 + sc_ref[0]) + sh_ref[0]
    hb = h.astype(_BF16)

    a = jnp.dot(hb, win_ref[:, 0:D_CONV], preferred_element_type=_F32)
    g = jnp.dot(hb, win_ref[:, D_CONV:2 * D_CONV], preferred_element_type=_F32)
    glu = a * _sigmoid(g)
    sub = lax.broadcasted_iota(jnp.int32, (CONV_HALO_VREGS, SUBLANES, LANES), 1)
    for c in range(CONV_LANE_GROUPS):
        for s in range(SUBLANES):
            xil_ref[c, pl.ds(halo_rows + s, seg, stride=SUBLANES), :] = (
                glu[s * seg:(s + 1) * seg, c * LANES:(c + 1) * LANES])
        tail = xil_ref[c, seg * SUBLANES:seg * SUBLANES + halo_rows, :]
        tail = pltpu.roll(tail.reshape(CONV_HALO_VREGS, SUBLANES, LANES), 1, 1)
        prev = pltpu.roll(prev_ref[c].reshape(CONV_HALO_VREGS, SUBLANES, LANES), 1, 1)
        xil_ref[c, 0:halo_rows, :] = jnp.where(sub == 0, prev, tail).reshape(halo_rows, LANES)

    def conv_block(c, r0, zero_row):
        accs = [jnp.broadcast_to(cb_ref[c] + zero_row, (CONV_ROWS, LANES))] + [None] * (CONV_ACCS - 1)
        for k in range(CONV_K):
            lo = r0 + (CONV_HALO_VREGS - (CONV_K - 1) + k) * SUBLANES
            tap = xil_ref[c, lo:lo + CONV_ROWS, :]
            prod = tap * cw_ref[k, c]
            a_i = k % CONV_ACCS
            accs[a_i] = prod if accs[a_i] is None else accs[a_i] + prod
        while len(accs) > 1:
            accs = [accs[p] + accs[p + 1] for p in range(0, len(accs), 2)]
        conv_ref[c, r0:r0 + CONV_ROWS, :] = accs[0]

    n_conv = CONV_LANE_GROUPS * CONV_ROW_STEPS
    convs_per_gate = n_conv // GATE_COL_BLOCKS
    for it in range(n_conv):
        if it % convs_per_gate == 0:
            n = it // convs_per_gate
            col = COL_GATE_A + n * GATE_COLS
            gate = _sigmoid(jnp.dot(hb, win_ref[:, col:col + GATE_COLS], preferred_element_type=_F32))
            gates_ref[n] = gate
            zero_row = jnp.where(gate[ts - 1:ts, GATE_COLS - LANES:] > 2.0, 1.0, 0.0)
        conv_block(it // CONV_ROW_STEPS, (it % CONV_ROW_STEPS) * CONV_ROWS, zero_row)

    u = jnp.dot(hb, win_ref[:, COL_POOL:COL_GATE_A], preferred_element_type=_F32)
    pe_ref[HALO:HALO + ts, :] = u
    n_ext = ts + HALO
    p2_ref[8:n_ext, :] = pe_ref[8:n_ext, :] + pe_ref[7:n_ext - 1, :]
    p4_ref[16:n_ext, :] = p2_ref[16:n_ext, :] + p2_ref[14:n_ext - 2, :]
    p8_ref[24:n_ext, :] = p4_ref[24:n_ext, :] + p4_ref[20:n_ext - 4, :]

    half = GATE_COL_BLOCKS // 2
    for rb in range(ts // TAIL_ROWS):
        r0 = rb * TAIL_ROWS
        rows = slice(r0, r0 + TAIL_ROWS)
        ext = slice(HALO + r0, HALO + r0 + TAIL_ROWS)
        v = jnp.concatenate(
            [jnp.concatenate([conv_ref[c, pl.ds(s, seg, stride=SUBLANES), :]
                              for s in range(r0 // seg, (r0 + TAIL_ROWS) // seg)], axis=0)
             for c in range(CONV_LANE_GROUPS)], axis=1)
        mu = jnp.mean(v, axis=-1, keepdims=True)
        vc = v - mu
        var = jnp.mean(vc * vc, axis=-1, keepdims=True)
        v = _silu(vc * lax.rsqrt(var + EPS) * lng_ref[...] + lnb_ref[...])
        y_a = jnp.dot(v.astype(_BF16), wco_ref[...], preferred_element_type=_F32) + bco_ref[...]

        pos = j * ts + r0 + lax.broadcasted_iota(jnp.int32, (TAIL_ROWS, 1), 0) + 1
        y_b_parts = []
        for gi, w in enumerate(POOL_WINDOWS):
            cs = slice(gi * POOL_GROUP, (gi + 1) * POOL_GROUP)
            if w == 2:
                win_sum = p2_ref[ext, cs]
            elif w == 4:
                win_sum = p4_ref[ext, cs]
            elif w == 8:
                win_sum = p8_ref[ext, cs]
            else:
                win_sum = p8_ref[ext, cs] + p8_ref[HALO + r0 - 8:HALO + r0 - 8 + TAIL_ROWS, cs]
            count = jnp.minimum(pos, w).astype(_F32)
            pooled = win_sum / count - pe_ref[ext, cs]
            y_b_parts.append(jnp.dot(pooled.astype(_BF16), pw_ref[gi], preferred_element_type=_F32))
        y_b = jnp.concatenate(y_b_parts, axis=-1) * ps_ref[...]

        gate_a = jnp.concatenate([gates_ref[n, rows, :] for n in range(half)], axis=1)
        gate_b = jnp.concatenate([gates_ref[n, rows, :] for n in range(half, GATE_COL_BLOCKS)], axis=1)
        mixed = gate_a * y_a + gate_b * y_b
        out = jnp.dot(mixed.astype(_BF16), wout_ref[...], preferred_element_type=_F32)
        o_ref[0, rows, :] = x_ref[0, rows, :] + gt_ref[0] * out


def _mixer(x, sh, sc, gt, layer, norm_g, w_in, conv_w, conv_b, ln_g, ln_b, w_co, b_co, pool_w, pool_scale,
           w_out):
    b, s, d = x.shape
    ts = SEQ_TILE
    const2 = lambda bi, j: (0, 0)
    const3 = lambda bi, j: (0, 0, 0)
    row = lambda a: a.reshape(1, -1)

    def layer_spec(a):
        return pl.BlockSpec((None,) + a.shape[1:], lambda bi, j: (layer,) + (0,) * (a.ndim - 1))
    mod_spec = pl.BlockSpec((1, 1, d), lambda bi, j: (bi, 0, 0))
    conv_w = conv_w.reshape(CONV_K, CONV_LANE_GROUPS, 1, LANES)
    conv_b = conv_b.reshape(CONV_LANE_GROUPS, 1, LANES)
    return pl.pallas_call(
        _mixer_kernel,
        grid=(b, s // ts),
        in_specs=[
            pl.BlockSpec((1, ts, d), lambda bi, j: (bi, j, 0)),
            mod_spec, mod_spec, mod_spec,
            pl.BlockSpec((1, d), const2),
            layer_spec(w_in),
            pl.BlockSpec(conv_w.shape, lambda bi, j: (0, 0, 0, 0)),
            pl.BlockSpec(conv_b.shape, const3),
            pl.BlockSpec((1, D_CONV), const2),
            pl.BlockSpec((1, D_CONV), const2),
            layer_spec(w_co),
            pl.BlockSpec((1, d), const2),
            layer_spec(pool_w),
            pl.BlockSpec((1, d), const2),
            layer_spec(w_out),
        ],
        out_specs=pl.BlockSpec((1, ts, d), lambda bi, j: (bi, j, 0)),
        out_shape=jax.ShapeDtypeStruct(x.shape, _F32),
        scratch_shapes=[
            pltpu.VMEM((GATE_COL_BLOCKS, ts, GATE_COLS), _F32),
            pltpu.VMEM((CONV_LANE_GROUPS, ts + CONV_HALO_VREGS * SUBLANES, LANES), _F32),
            pltpu.VMEM((CONV_LANE_GROUPS, CONV_HALO_VREGS * SUBLANES, LANES), _F32),
            pltpu.VMEM((CONV_LANE_GROUPS, ts, LANES), _F32),
            pltpu.VMEM((ts + HALO, D_POOL), _F32),
            pltpu.VMEM((ts + HALO, D_POOL), _F32),
            pltpu.VMEM((ts + HALO, D_POOL), _F32),
            pltpu.VMEM((ts + HALO, D_POOL), _F32),
        ],
        compiler_params=pltpu.CompilerParams(
            dimension_semantics=("arbitrary", "arbitrary"), vmem_limit_bytes=VMEM_LIMIT),
        name="mixer",
    )(x, sh, sc, gt, row(norm_g), w_in, conv_w, conv_b, row(ln_g), row(ln_b), w_co, row(b_co),
      pool_w, row(pool_scale), w_out)


def _route_kernel(x_ref, sh_ref, sc_ref, ng_ref, wrt_ref, brt_ref, pos_ref, wgt_ref, seg_ref,
                  id_scr, rank_scr, wgt_scr, run_ref):
    tb = ROUTE_BLOCK
    j = pl.program_id(1)

    @pl.when(j == 0)
    def _():
        run_ref[...] = jnp.zeros_like(run_ref)

    h = _rms_norm(x_ref[...], ng_ref[...]) * (1.0 + sc_ref[0]) + sh_ref[0]
    nt = (((1,), (1,)), ((), ()))
    h_hi = h.astype(_BF16)
    h_lo = (h - h_hi.astype(_F32)).astype(_BF16)
    w = wrt_ref[...]
    w_hi = w.astype(_BF16)
    w_lo = (w - w_hi.astype(_F32)).astype(_BF16)
    part = lax.dot_general(jnp.concatenate([w_hi, w_lo], axis=0), h_hi, nt, preferred_element_type=_F32)
    logits = (part[0:ROUTER_ROWS] + part[ROUTER_ROWS:]
              + lax.dot_general(w_hi, h_lo, nt, preferred_element_type=_F32)
              + brt_ref[...])
    neg = jnp.float32(-jnp.inf)
    big = jnp.int32(N_EXPERTS)

    grow = lax.broadcasted_iota(jnp.int32, (SUBLANES, tb), 0)
    glog = jnp.where(grow < N_GROUPS, logits[GROUP_ROW0:GROUP_ROW0 + SUBLANES, :], neg)
    gmax = jnp.max(glog, axis=0, keepdims=True)
    p_group = 1.0 / jnp.sum(jnp.exp(glog - gmax), axis=0, keepdims=True)
    g_idx = jnp.min(jnp.where(glog == gmax, grow, big), axis=0, keepdims=True)

    erow = lax.broadcasted_iota(jnp.int32, (N_EXPERTS, tb), 0)
    el = jnp.where((erow // PER_GROUP) == g_idx, logits[0:N_EXPERTS, :], neg)
    m1 = jnp.max(el, axis=0, keepdims=True)
    i1 = jnp.min(jnp.where(el == m1, erow, big), axis=0, keepdims=True)
    el2 = jnp.where(erow == i1, neg, el)
    m2 = jnp.max(el2, axis=0, keepdims=True)
    i2 = jnp.min(jnp.where(el2 == m2, erow, big), axis=0, keepdims=True)
    r = jnp.exp(m2 - m1)
    w1 = p_group / (1.0 + r)
    w2 = w1 * r
    sel1 = erow == i1
    sel2 = erow == i2

    used = jnp.where(sel1 | sel2, 1.0, 0.0)
    tri_r = lax.broadcasted_iota(jnp.int32, (CUM_BLOCK, CUM_BLOCK), 0)
    tri_c = lax.broadcasted_iota(jnp.int32, (CUM_BLOCK, CUM_BLOCK), 1)
    upper = jnp.where(tri_r < tri_c, 1.0, 0.0).astype(_BF16)
    run = run_ref[:, 0:1]
    ranks = []
    for blk in range(tb // CUM_BLOCK):
        ub = used[:, blk * CUM_BLOCK:(blk + 1) * CUM_BLOCK]
        ranks.append(jnp.dot(ub.astype(_BF16), upper, preferred_element_type=_F32) + run)
        run = run + jnp.sum(ub, axis=1, keepdims=True)
    rank = jnp.concatenate(ranks, axis=1)
    run_ref[...] = jnp.broadcast_to(run, run_ref.shape)
    id_scr[j, 0:1, :] = i1
    id_scr[j, 1:2, :] = i2
    rank_scr[j, 0:1, :] = jnp.sum(jnp.where(sel1, rank, 0.0), axis=0, keepdims=True)
    rank_scr[j, 1:2, :] = jnp.sum(jnp.where(sel2, rank, 0.0), axis=0, keepdims=True)
    wgt_scr[j, 0:1, :] = w1
    wgt_scr[j, 1:2, :] = w2

    @pl.when(j == ROUTE_BLOCKS - 1)
    def _():
        cnt = jnp.concatenate([run_ref[...], jnp.zeros((LANES - N_EXPERTS, LANES), _F32)], axis=0)
        low_r = lax.broadcasted_iota(jnp.int32, (LANES, LANES), 0)
        low_c = lax.broadcasted_iota(jnp.int32, (LANES, LANES), 1)
        lower = jnp.where(low_c < low_r, 1.0, 0.0)
        off = jnp.dot(lower, cnt, preferred_element_type=_F32, precision=lax.Precision.HIGHEST)
        seg_ref[0, 0:N_EXPERTS, :] = off[0:N_EXPERTS, :].astype(jnp.int32)
        seg_ref[0, N_EXPERTS:2 * N_EXPERTS, :] = cnt[0:N_EXPERTS, :].astype(jnp.int32)
        off_col = off[0:N_EXPERTS, 0:1]
        for blk in range(ROUTE_BLOCKS):
            ls = slice(blk * tb, (blk + 1) * tb)
            for k in range(TOP_K):
                start = jnp.sum(jnp.where(erow == id_scr[blk, k:k + 1, :], off_col, 0.0),
                                axis=0, keepdims=True)
                slot = start + rank_scr[blk, k:k + 1, :]
                pos_ref[0, k:k + 1, ls] = (slot * ROW_VREGS).astype(jnp.int32)
                wgt_ref[0, k:k + 1, ls] = wgt_scr[blk, k:k + 1, :]


def _route(xf, sh, sc, norm_g, w_rt, b_rt, per_b):
    t, d = xf.shape
    tm, tb, nb = MOE_TILE, ROUTE_BLOCK, ROUTE_BLOCKS
    n_tiles = t // tm
    const2 = lambda i, j: (0, 0)
    mod_spec = pl.BlockSpec((1, 1, d), lambda i, j: (i // per_b, 0, 0))
    out3 = lambda i, j: (i, 0, 0)
    return pl.pallas_call(
        _route_kernel,
        grid=(n_tiles, nb),
        in_specs=[
            pl.BlockSpec((tb, d), lambda i, j: (i * nb + j, 0)),
            mod_spec, mod_spec,
            pl.BlockSpec((1, d), const2),
            pl.BlockSpec((ROUTER_ROWS, d), const2),
            pl.BlockSpec((ROUTER_ROWS, 1), const2),
        ],
        out_specs=[
            pl.BlockSpec((1, TOP_K, tm), out3),
            pl.BlockSpec((1, TOP_K, tm), out3),
            pl.BlockSpec((1, 2 * N_EXPERTS, LANES), out3),
        ],
        out_shape=[
            jax.ShapeDtypeStruct((n_tiles, TOP_K, tm), jnp.int32),
            jax.ShapeDtypeStruct((n_tiles, TOP_K, tm), _F32),
            jax.ShapeDtypeStruct((n_tiles, 2 * N_EXPERTS, LANES), jnp.int32),
        ],
        scratch_shapes=[
            pltpu.VMEM((nb, TOP_K, tb), jnp.int32),
            pltpu.VMEM((nb, TOP_K, tb), _F32),
            pltpu.VMEM((nb, TOP_K, tb), _F32),
            pltpu.VMEM((N_EXPERTS, LANES), _F32),
        ],
        compiler_params=pltpu.CompilerParams(
            dimension_semantics=("arbitrary", "arbitrary"), vmem_limit_bytes=VMEM_LIMIT),
        name="route",
    )(xf, sh, sc, norm_g.reshape(1, d), w_rt, b_rt)


def _row_ds(offset):
    return pl.ds(pl.multiple_of(offset, ROW_VREGS), ROW_VREGS)


def _moe_kernel(pos_s, wgt_s, seg_s,
                x_ref, sh_ref, sc_ref, gt_ref, ng_ref, fg_ref, wg_hbm, wu_hbm, wd_hbm, o_ref,
                tm_ref, srt_ref, wgu_buf, wd_buf, sem, *, layer, final_norm):
    tm, tb, nb, ch = MOE_TILE, MOE_BLOCK, MOE_BLOCKS, MOE_CHUNK
    i = pl.program_id(0)
    j = pl.program_id(1)
    base = i * (TOP_K * tm) + (j % nb) * tb

    def weight_copies(e, slot):
        return (pltpu.make_async_copy(wg_hbm.at[layer, e], wgu_buf.at[slot, :, 0:D_EXPERT], sem.at[0, slot]),
                pltpu.make_async_copy(wu_hbm.at[layer, e], wgu_buf.at[slot, :, D_EXPERT:], sem.at[1, slot]),
                pltpu.make_async_copy(wd_hbm.at[layer, e], wd_buf.at[slot], sem.at[2, slot]))

    @pl.when(j == 0)
    def _():
        for e0 in range(W_SLOTS - 1):
            for cp in weight_copies(e0, e0):
                cp.start()
        srt_ref[TOP_K * tm * ROW_VREGS:, :] = jnp.zeros((MOE_PAD_ROWS * ROW_VREGS, LANES), _F32)

    @pl.when(j < nb)
    def _():
        h = _rms_norm(x_ref[...], ng_ref[...]) * (1.0 + sc_ref[0]) + sh_ref[0]
        for s in range(ROW_VREGS):
            tm_ref[pl.ds(s, tb, stride=ROW_VREGS), :] = h[:, s * LANES:(s + 1) * LANES]

        def dispatch(it, carry):
            t0 = it * UNROLL
            rows = [tm_ref[_row_ds((t0 + u) * ROW_VREGS), :] for u in range(UNROLL)]
            for u in range(UNROLL):
                for k in range(TOP_K):
                    srt_ref[_row_ds(pos_s[base + k * tm + t0 + u]), :] = rows[u]
            return carry

        lax.fori_loop(0, tb // UNROLL, dispatch, 0)

    @pl.when(j == nb)
    def _():
        def expert(e, carry):
            slot = e % W_SLOTS
            for cp in weight_copies(e, slot):
                cp.wait()

            @pl.when(e + W_SLOTS - 1 < N_EXPERTS)
            def _():
                for cp in weight_copies(e + W_SLOTS - 1, (e + W_SLOTS - 1) % W_SLOTS):
                    cp.start()

            off = seg_s[i * (2 * N_EXPERTS) + e]
            cnt = seg_s[i * (2 * N_EXPERTS) + N_EXPERTS + e]

            def run_rows(first_row, n_sub):
                subs = range(n_sub)
                row0 = [first_row + sub * MOE_SUB for sub in subs]
                r0 = [(off + r) * ROW_VREGS for r in row0]
                xs = [jnp.concatenate([srt_ref[pl.ds(r0[sub] + s, MOE_SUB, stride=ROW_VREGS), :]
                                       for s in range(ROW_VREGS)], axis=1) for sub in subs]
                ys = []
                hgus = [jnp.dot(xs[sub].astype(_BF16), wgu_buf[slot], preferred_element_type=_F32)
                        for sub in subs]
                for sub in subs:
                    hgu = hgus[sub]
                    act = _silu(hgu[:, :D_EXPERT]) * hgu[:, D_EXPERT:]
                    y = jnp.dot(act.astype(_BF16), wd_buf[slot], preferred_element_type=_F32)
                    valid = lax.broadcasted_iota(jnp.int32, (MOE_SUB, 1), 0) < (cnt - row0[sub])
                    ys.append(jnp.where(valid, y, xs[sub]))
                for sub in subs:
                    for s in range(ROW_VREGS):
                        srt_ref[pl.ds(r0[sub] + s, MOE_SUB, stride=ROW_VREGS), :] = (
                            ys[sub][:, s * LANES:(s + 1) * LANES])

            def chunk(c, inner):
                run_rows(c * ch, ch // MOE_SUB)
                return inner

            n_whole = jnp.maximum(cnt - MOE_TAIL_SUBS * MOE_SUB + ch - 1, 0) // ch
            lax.fori_loop(0, n_whole, chunk, 0)
            rest = cnt - n_whole * ch
            for n_sub in range(1, MOE_TAIL_SUBS + 1):
                @pl.when((rest > (n_sub - 1) * MOE_SUB) & (rest <= n_sub * MOE_SUB))
                def _(n_sub=n_sub):
                    run_rows(n_whole * ch, n_sub)
            return carry

        lax.fori_loop(0, N_EXPERTS, expert, 0)

    @pl.when(j >= nb)
    def _():
        def combine(it, carry):
            t0 = it * UNROLL
            outs = []
            for u in range(UNROLL):
                t = base + t0 + u
                y0 = srt_ref[_row_ds(pos_s[t]), :]
                y1 = srt_ref[_row_ds(pos_s[t + tm]), :]
                outs.append(wgt_s[t] * y0 + wgt_s[t + tm] * y1)
            for u in range(UNROLL):
                tm_ref[_row_ds((t0 + u) * ROW_VREGS), :] = outs[u]
            return carry

        lax.fori_loop(0, tb // UNROLL, combine, 0)

        moe = jnp.concatenate(
            [tm_ref[pl.ds(s, tb, stride=ROW_VREGS), :] for s in range(ROW_VREGS)], axis=1)
        y = x_ref[...] + gt_ref[0] * moe
        if final_norm:
            y = _rms_norm(y, fg_ref[...])
        o_ref[...] = y


def _moe(x, sh, sc, gt, layer, norm_g, w_rt, b_rt, w_g, w_u, w_d, final_g, final_norm):
    b, s, d = x.shape
    t = b * s
    tm, tb, nb = MOE_TILE, MOE_BLOCK, MOE_BLOCKS
    per_b = s // tm
    n_tiles = t // tm
    xf = x.reshape(t, d)
    pos, wgt, seg = _route(xf, sh, sc, norm_g, w_rt, b_rt, per_b)
    const2 = lambda i, j, *_: (0, 0)
    mod_spec = pl.BlockSpec((1, 1, d), lambda i, j, *_: (i // per_b, 0, 0))
    out = pl.pallas_call(
        functools.partial(_moe_kernel, layer=layer, final_norm=final_norm),
        grid_spec=pltpu.PrefetchScalarGridSpec(
            num_scalar_prefetch=3,
            grid=(n_tiles, 2 * nb),
            in_specs=[
                pl.BlockSpec((tb, d), lambda i, j, *_: (i * nb + j % nb, 0)),
                mod_spec, mod_spec, mod_spec,
                pl.BlockSpec((1, d), const2),
                pl.BlockSpec((1, d), const2),
                pl.BlockSpec(memory_space=pl.ANY),
                pl.BlockSpec(memory_space=pl.ANY),
                pl.BlockSpec(memory_space=pl.ANY),
            ],
            out_specs=pl.BlockSpec((tb, d), lambda i, j, *_: (i * nb + jnp.maximum(j - nb, 0), 0)),
            scratch_shapes=[
                pltpu.VMEM((tb * ROW_VREGS, LANES), _F32),
                pltpu.VMEM(((TOP_K * tm + MOE_PAD_ROWS) * ROW_VREGS, LANES), _F32),
                pltpu.VMEM((W_SLOTS, d, 2 * D_EXPERT), _BF16),
                pltpu.VMEM((W_SLOTS, D_EXPERT, d), _BF16),
                pltpu.SemaphoreType.DMA((3, W_SLOTS)),
            ],
        ),
        out_shape=jax.ShapeDtypeStruct((t, d), _F32),
        compiler_params=pltpu.CompilerParams(
            dimension_semantics=("arbitrary", "arbitrary"), vmem_limit_bytes=VMEM_LIMIT),
        name="moe",
    )(pos.reshape(-1), wgt.reshape(-1), seg[:, :, 0].reshape(-1),
      xf, sh, sc, gt, norm_g.reshape(1, d), final_g.reshape(1, d), w_g, w_u, w_d)
    return out.reshape(b, s, d)


def _router_params(w_rg, b_rg, w_re, b_re):
    d = w_rg.shape[0]
    pad = ROUTER_ROWS - N_EXPERTS - N_GROUPS
    w = jnp.concatenate([w_re.T, w_rg.T, jnp.zeros((pad, d), _F32)], axis=0)
    b = jnp.concatenate([b_re, b_rg, jnp.zeros((pad,), _F32)]).reshape(ROUTER_ROWS, 1)
    return w, b


def kernel(x, c, mixer_norm_g, w_ada, b_ada, w_in, conv_w, conv_b, conv_ln_g, conv_ln_b, w_conv_out,
           b_conv_out, pool_w, pool_scale, w_out, ffn_norm_g, w_router_group, b_router_group,
           w_router_expert, b_router_expert, w_expert_gate, w_expert_up, w_expert_down, final_norm_g):
    depth = w_ada.shape[0]
    bsz = x.shape[0]
    mod = _ada(c, w_ada, b_ada)
    mod = mod.reshape(depth, bsz, N_MOD, 1, D_MODEL)
    w_in_b, w_co_b, pool_w_b, w_out_b = (a.astype(_BF16) for a in (w_in, w_conv_out, pool_w, w_out))
    w_eg_b, w_eu_b, w_ed_b = (a.astype(_BF16) for a in (w_expert_gate, w_expert_up, w_expert_down))
    for l in range(depth):
        sh1, sc1, g1, sh2, sc2, g2 = (mod[l, :, i] for i in range(N_MOD))
        x = _mixer(x, sh1, sc1, g1, l, mixer_norm_g[l], w_in_b, conv_w[l], conv_b[l],
                   conv_ln_g[l], conv_ln_b[l], w_co_b, b_conv_out[l], pool_w_b, pool_scale[l], w_out_b)
        w_rt, b_rt = _router_params(w_router_group[l], b_router_group[l],
                                    w_router_expert[l], b_router_expert[l])
        x = _moe(x, sh2, sc2, g2, l, ffn_norm_g[l], w_rt, b_rt, w_eg_b, w_eu_b, w_ed_b,
                 final_norm_g, final_norm=(l == depth - 1))
    return x
```

```python
import functools

import jax
import jax.numpy as jnp
from jax import lax
from jax.experimental import pallas as pl
from jax.experimental.pallas import tpu as pltpu

D_MODEL = 1024
D_CONV = 512
D_POOL = 512
CONV_K = 31
POOL_WINDOWS = (2, 4, 8, 16)
POOL_GROUP = 128
POOL_OUT_GROUP = 256
N_GROUPS = 4
PER_GROUP = 8
N_EXPERTS = 32
TOP_K = 2
D_EXPERT = 256
N_MOD = 6
EPS = 1e-6

COL_POOL = 2 * D_CONV
COL_GATE_A = COL_POOL + D_POOL
COL_GATE_B = COL_GATE_A + D_MODEL
D_IN = COL_GATE_B + D_MODEL

LANES = 128
SUBLANES = 8
ROW_VREGS = D_MODEL // LANES
HALO = 32
CONV_HALO_VREGS = 32
SEQ_TILE = 512
CONV_ROWS = 128
CONV_ROW_STEPS = SEQ_TILE // CONV_ROWS
CONV_LANE_GROUPS = D_CONV // LANES
CONV_ACCS = 4
TAIL_ROWS = 256
GATE_COLS = 256
GATE_COL_BLOCKS = (D_IN - COL_GATE_A) // GATE_COLS
MOE_TILE = 4096
MOE_BLOCK = 512
MOE_BLOCKS = MOE_TILE // MOE_BLOCK
ROUTE_BLOCK = 2048
ROUTE_BLOCKS = MOE_TILE // ROUTE_BLOCK
MOE_CHUNK = 256
MOE_SUB = 128
MOE_TAIL_SUBS = 3
MOE_PAD_ROWS = MOE_SUB
W_SLOTS = 4
CUM_BLOCK = 256
ROUTER_ROWS = 48
GROUP_ROW0 = N_EXPERTS
UNROLL = 32
VMEM_LIMIT = 56 * 1024 * 1024

_F32 = jnp.float32
_BF16 = jnp.bfloat16
assert ROW_VREGS == SUBLANES


def _sigmoid(v):
    return 1.0 / (1.0 + jnp.exp(-v))


def _silu(v):
    return v * _sigmoid(v)


def _rms_norm(v, g):
    return v * lax.rsqrt(jnp.mean(v * v, axis=-1, keepdims=True) + EPS) * g


def _ada_kernel(c_ref, w_ref, b_ref, o_ref):
    c_act = _silu(c_ref[...])
    o_ref[0] = jnp.dot(c_act, w_ref[0], preferred_element_type=_F32,
                       precision=lax.Precision.HIGHEST) + b_ref[0]


def _ada(c, w_ada, b_ada):
    depth, d, n = w_ada.shape
    tn = 1024
    return pl.pallas_call(
        _ada_kernel,
        grid=(depth, n // tn),
        in_specs=[
            pl.BlockSpec(c.shape, lambda l, j: (0, 0)),
            pl.BlockSpec((1, d, tn), lambda l, j: (l, 0, j)),
            pl.BlockSpec((1, 1, tn), lambda l, j: (l, 0, j)),
        ],
        out_specs=pl.BlockSpec((1, c.shape[0], tn), lambda l, j: (l, 0, j)),
        out_shape=jax.ShapeDtypeStruct((depth, c.shape[0], n), _F32),
        name="ada_mod",
    )(c, w_ada, b_ada.reshape(depth, 1, n))


def _mixer_kernel(x_ref, sh_ref, sc_ref, gt_ref, ng_ref, win_ref, cw_ref, cb_ref, lng_ref, lnb_ref,
                  wco_ref, bco_ref, pw_ref, ps_ref, wout_ref, o_ref,
                  gates_ref, xil_ref, prev_ref, conv_ref, pe_ref, p2_ref, p4_ref, p8_ref):
    ts = SEQ_TILE
    seg = ts // SUBLANES
    halo_rows = CONV_HALO_VREGS * SUBLANES
    j = pl.program_id(1)

    @pl.when(j == 0)
    def _():
        prev_ref[...] = jnp.zeros_like(prev_ref)
        pe_ref[0:HALO, :] = jnp.zeros((HALO, D_POOL), _F32)

    @pl.when(j > 0)
    def _():
        prev_ref[...] = xil_ref[:, seg * SUBLANES:seg * SUBLANES + halo_rows, :]
        pe_ref[0:HALO, :] = pe_ref[ts:ts + HALO, :]

    xt = x_ref[0]
    h = _rms_norm(xt, ng_ref[...]) * (1.0 + sc_ref[0]) + sh_ref[0]
    hb = h.astype(_BF16)

    a = jnp.dot(hb, win_ref[:, 0:D_CONV], preferred_element_type=_F32)
    g = jnp.dot(hb, win_ref[:, D_CONV:2 * D_CONV], preferred_element_type=_F32)
    glu = a * _sigmoid(g)
    sub = lax.broadcasted_iota(jnp.int32, (CONV_HALO_VREGS, SUBLANES, LANES), 1)
    for c in range(CONV_LANE_GROUPS):
        for s in range(SUBLANES):
            xil_ref[c, pl.ds(halo_rows + s, seg, stride=SUBLANES), :] = (
                glu[s * seg:(s + 1) * seg, c * LANES:(c + 1) * LANES])
        tail = xil_ref[c, seg * SUBLANES:seg * SUBLANES + halo_rows, :]
        tail = pltpu.roll(tail.reshape(CONV_HALO_VREGS, SUBLANES, LANES), 1, 1)
        prev = pltpu.roll(prev_ref[c].reshape(CONV_HALO_VREGS, SUBLANES, LANES), 1, 1)
        xil_ref[c, 0:halo_rows, :] = jnp.where(sub == 0, prev, tail).reshape(halo_rows, LANES)

    def conv_block(c, r0, zero_row):
        accs = [jnp.broadcast_to(cb_ref[c] + zero_row, (CONV_ROWS, LANES))] + [None] * (CONV_ACCS - 1)
        for k in range(CONV_K):
            lo = r0 + (CONV_HALO_VREGS - (CONV_K - 1) + k) * SUBLANES
            tap = xil_ref[c, lo:lo + CONV_ROWS, :]
            prod = tap * cw_ref[k, c]
            a_i = k % CONV_ACCS
            accs[a_i] = prod if accs[a_i] is None else accs[a_i] + prod
        while len(accs) > 1:
            accs = [accs[p] + accs[p + 1] for p in range(0, len(accs), 2)]
        conv_ref[c, r0:r0 + CONV_ROWS, :] = accs[0]

    n_conv = CONV_LANE_GROUPS * CONV_ROW_STEPS
    convs_per_gate = n_conv // GATE_COL_BLOCKS
    for it in range(n_conv):
        if it % convs_per_gate == 0:
            n = it // convs_per_gate
            col = COL_GATE_A + n * GATE_COLS
            gate = _sigmoid(jnp.dot(hb, win_ref[:, col:col + GATE_COLS], preferred_element_type=_F32))
            gates_ref[n] = gate
            zero_row = jnp.where(gate[ts - 1:ts, GATE_COLS - LANES:] > 2.0, 1.0, 0.0)
        conv_block(it // CONV_ROW_STEPS, (it % CONV_ROW_STEPS) * CONV_ROWS, zero_row)

    u = jnp.dot(hb, win_ref[:, COL_POOL:COL_GATE_A], preferred_element_type=_F32)
    pe_ref[HALO:HALO + ts, :] = u
    n_ext = ts + HALO
    p2_ref[8:n_ext, :] = pe_ref[8:n_ext, :] + pe_ref[7:n_ext - 1, :]
    p4_ref[16:n_ext, :] = p2_ref[16:n_ext, :] + p2_ref[14:n_ext - 2, :]
    p8_ref[24:n_ext, :] = p4_ref[24:n_ext, :] + p4_ref[20:n_ext - 4, :]

    half = GATE_COL_BLOCKS // 2
    for rb in range(ts // TAIL_ROWS):
        r0 = rb * TAIL_ROWS
        rows = slice(r0, r0 + TAIL_ROWS)
        ext = slice(HALO + r0, HALO + r0 + TAIL_ROWS)
        v = jnp.concatenate(
            [jnp.concatenate([conv_ref[c, pl.ds(s, seg, stride=SUBLANES), :]
                              for s in range(r0 // seg, (r0 + TAIL_ROWS) // seg)], axis=0)
             for c in range(CONV_LANE_GROUPS)], axis=1)
        mu = jnp.mean(v, axis=-1, keepdims=True)
        vc = v - mu
        var = jnp.mean(vc * vc, axis=-1, keepdims=True)
        v = _silu(vc * lax.rsqrt(var + EPS) * lng_ref[...] + lnb_ref[...])
        y_a = jnp.dot(v.astype(_BF16), wco_ref[...], preferred_element_type=_F32) + bco_ref[...]

        pos = j * ts + r0 + lax.broadcasted_iota(jnp.int32, (TAIL_ROWS, 1), 0) + 1
        y_b_parts = []
        for gi, w in enumerate(POOL_WINDOWS):
            cs = slice(gi * POOL_GROUP, (gi + 1) * POOL_GROUP)
            if w == 2:
                win_sum = p2_ref[ext, cs]
            elif w == 4:
                win_sum = p4_ref[ext, cs]
            elif w == 8:
                win_sum = p8_ref[ext, cs]
            else:
                win_sum = p8_ref[ext, cs] + p8_ref[HALO + r0 - 8:HALO + r0 - 8 + TAIL_ROWS, cs]
            count = jnp.minimum(pos, w).astype(_F32)
            pooled = win_sum / count - pe_ref[ext, cs]
            y_b_parts.append(jnp.dot(pooled.astype(_BF16), pw_ref[gi], preferred_element_type=_F32))
        y_b = jnp.concatenate(y_b_parts, axis=-1) * ps_ref[...]

        gate_a = jnp.concatenate([gates_ref[n, rows, :] for n in range(half)], axis=1)
        gate_b = jnp.concatenate([gates_ref[n, rows, :] for n in range(half, GATE_COL_BLOCKS)], axis=1)
        mixed = gate_a * y_a + gate_b * y_b
        out = jnp.dot(mixed.astype(_BF16), wout_ref[...], preferred_element_type=_F32)
        o_ref[0, rows, :] = x_ref[0, rows, :] + gt_ref[0] * out


def _mixer(x, sh, sc, gt, layer, norm_g, w_in, conv_w, conv_b, ln_g, ln_b, w_co, b_co, pool_w, pool_scale,
           w_out):
    b, s, d = x.shape
    ts = SEQ_TILE
    const2 = lambda bi, j: (0, 0)
    const3 = lambda bi, j: (0, 0, 0)
    row = lambda a: a.reshape(1, -1)

    def layer_spec(a):
        return pl.BlockSpec((None,) + a.shape[1:], lambda bi, j: (layer,) + (0,) * (a.ndim - 1))
    mod_spec = pl.BlockSpec((1, 1, d), lambda bi, j: (bi, 0, 0))
    conv_w = conv_w.reshape(CONV_K, CONV_LANE_GROUPS, 1, LANES)
    conv_b = conv_b.reshape(CONV_LANE_GROUPS, 1, LANES)
    return pl.pallas_call(
        _mixer_kernel,
        grid=(b, s // ts),
        in_specs=[
            pl.BlockSpec((1, ts, d), lambda bi, j: (bi, j, 0)),
            mod_spec, mod_spec, mod_spec,
            pl.BlockSpec((1, d), const2),
            layer_spec(w_in),
            pl.BlockSpec(conv_w.shape, lambda bi, j: (0, 0, 0, 0)),
            pl.BlockSpec(conv_b.shape, const3),
            pl.BlockSpec((1, D_CONV), const2),
            pl.BlockSpec((1, D_CONV), const2),
            layer_spec(w_co),
            pl.BlockSpec((1, d), const2),
            layer_spec(pool_w),
            pl.BlockSpec((1, d), const2),
            layer_spec(w_out),
        ],
        out_specs=pl.BlockSpec((1, ts, d), lambda bi, j: (bi, j, 0)),
        out_shape=jax.ShapeDtypeStruct(x.shape, _F32),
        scratch_shapes=[
            pltpu.VMEM((GATE_COL_BLOCKS, ts, GATE_COLS), _F32),
            pltpu.VMEM((CONV_LANE_GROUPS, ts + CONV_HALO_VREGS * SUBLANES, LANES), _F32),
            pltpu.VMEM((CONV_LANE_GROUPS, CONV_HALO_VREGS * SUBLANES, LANES), _F32),
            pltpu.VMEM((CONV_LANE_GROUPS, ts, LANES), _F32),
            pltpu.VMEM((ts + HALO, D_POOL), _F32),
            pltpu.VMEM((ts + HALO, D_POOL), _F32),
            pltpu.VMEM((ts + HALO, D_POOL), _F32),
            pltpu.VMEM((ts + HALO, D_POOL), _F32),
        ],
        compiler_params=pltpu.CompilerParams(
            dimension_semantics=("arbitrary", "arbitrary"), vmem_limit_bytes=VMEM_LIMIT),
        name="mixer",
    )(x, sh, sc, gt, row(norm_g), w_in, conv_w, conv_b, row(ln_g), row(ln_b), w_co, row(b_co),
      pool_w, row(pool_scale), w_out)


def _route_kernel(x_ref, sh_ref, sc_ref, ng_ref, wrt_ref, brt_ref, pos_ref, wgt_ref, seg_ref,
                  id_scr, rank_scr, wgt_scr, run_ref):
    tb = ROUTE_BLOCK
    j = pl.program_id(1)

    @pl.when(j == 0)
    def _():
        run_ref[...] = jnp.zeros_like(run_ref)

    h = _rms_norm(x_ref[...], ng_ref[...]) * (1.0 + sc_ref[0]) + sh_ref[0]
    nt = (((1,), (1,)), ((), ()))
    h_hi = h.astype(_BF16)
    h_lo = (h - h_hi.astype(_F32)).astype(_BF16)
    w = wrt_ref[...]
    w_hi = w.astype(_BF16)
    w_lo = (w - w_hi.astype(_F32)).astype(_BF16)
    part = lax.dot_general(jnp.concatenate([w_hi, w_lo], axis=0), h_hi, nt, preferred_element_type=_F32)
    logits = (part[0:ROUTER_ROWS] + part[ROUTER_ROWS:]
              + lax.dot_general(w_hi, h_lo, nt, preferred_element_type=_F32)
              + brt_ref[...])
    neg = jnp.float32(-jnp.inf)
    big = jnp.int32(N_EXPERTS)

    grow = lax.broadcasted_iota(jnp.int32, (SUBLANES, tb), 0)
    glog = jnp.where(grow < N_GROUPS, logits[GROUP_ROW0:GROUP_ROW0 + SUBLANES, :], neg)
    gmax = jnp.max(glog, axis=0, keepdims=True)
    p_group = 1.0 / jnp.sum(jnp.exp(glog - gmax), axis=0, keepdims=True)
    g_idx = jnp.min(jnp.where(glog == gmax, grow, big), axis=0, keepdims=True)

    erow = lax.broadcasted_iota(jnp.int32, (N_EXPERTS, tb), 0)
    el = jnp.where((erow // PER_GROUP) == g_idx, logits[0:N_EXPERTS, :], neg)
    m1 = jnp.max(el, axis=0, keepdims=True)
    i1 = jnp.min(jnp.where(el == m1, erow, big), axis=0, keepdims=True)
    el2 = jnp.where(erow == i1, neg, el)
    m2 = jnp.max(el2, axis=0, keepdims=True)
    i2 = jnp.min(jnp.where(el2 == m2, erow, big), axis=0, keepdims=True)
    r = jnp.exp(m2 - m1)
    w1 = p_group / (1.0 + r)
    w2 = w1 * r
    sel1 = erow == i1
    sel2 = erow == i2

    used = jnp.where(sel1 | sel2, 1.0, 0.0)
    tri_r = lax.broadcasted_iota(jnp.int32, (CUM_BLOCK, CUM_BLOCK), 0)
    tri_c = lax.broadcasted_iota(jnp.int32, (CUM_BLOCK, CUM_BLOCK), 1)
    upper = jnp.where(tri_r < tri_c, 1.0, 0.0).astype(_BF16)
    run = run_ref[:, 0:1]
    ranks = []
    for blk in range(tb // CUM_BLOCK):
        ub = used[:, blk * CUM_BLOCK:(blk + 1) * CUM_BLOCK]
        ranks.append(jnp.dot(ub.astype(_BF16), upper, preferred_element_type=_F32) + run)
        run = run + jnp.sum(ub, axis=1, keepdims=True)
    rank = jnp.concatenate(ranks, axis=1)
    run_ref[...] = jnp.broadcast_to(run, run_ref.shape)
    id_scr[j, 0:1, :] = i1
    id_scr[j, 1:2, :] = i2
    rank_scr[j, 0:1, :] = jnp.sum(jnp.where(sel1, rank, 0.0), axis=0, keepdims=True)
    rank_scr[j, 1:2, :] = jnp.sum(jnp.where(sel2, rank, 0.0), axis=0, keepdims=True)
    wgt_scr[j, 0:1, :] = w1
    wgt_scr[j, 1:2, :] = w2

    @pl.when(j == ROUTE_BLOCKS - 1)
    def _():
        cnt = jnp.concatenate([run_ref[...], jnp.zeros((LANES - N_EXPERTS, LANES), _F32)], axis=0)
        low_r = lax.broadcasted_iota(jnp.int32, (LANES, LANES), 0)
        low_c = lax.broadcasted_iota(jnp.int32, (LANES, LANES), 1)
        lower = jnp.where(low_c < low_r, 1.0, 0.0)
        off = jnp.dot(lower, cnt, preferred_element_type=_F32, precision=lax.Precision.HIGHEST)
        seg_ref[0, 0:N_EXPERTS, :] = off[0:N_EXPERTS, :].astype(jnp.int32)
        seg_ref[0, N_EXPERTS:2 * N_EXPERTS, :] = cnt[0:N_EXPERTS, :].astype(jnp.int32)
        off_col = off[0:N_EXPERTS, 0:1]
        for blk in range(ROUTE_BLOCKS):
            ls = slice(blk * tb, (blk + 1) * tb)
            for k in range(TOP_K):
                start = jnp.sum(jnp.where(erow == id_scr[blk, k:k + 1, :], off_col, 0.0),
                                axis=0, keepdims=True)
                slot = start + rank_scr[blk, k:k + 1, :]
                pos_ref[0, k:k + 1, ls] = (slot * ROW_VREGS).astype(jnp.int32)
                wgt_ref[0, k:k + 1, ls] = wgt_scr[blk, k:k + 1, :]


def _route(xf, sh, sc, norm_g, w_rt, b_rt, per_b):
    t, d = xf.shape
    tm, tb, nb = MOE_TILE, ROUTE_BLOCK, ROUTE_BLOCKS
    n_tiles = t // tm
    const2 = lambda i, j: (0, 0)
    mod_spec = pl.BlockSpec((1, 1, d), lambda i, j: (i // per_b, 0, 0))
    out3 = lambda i, j: (i, 0, 0)
    return pl.pallas_call(
        _route_kernel,
        grid=(n_tiles, nb),
        in_specs=[
            pl.BlockSpec((tb, d), lambda i, j: (i * nb + j, 0)),
            mod_spec, mod_spec,
            pl.BlockSpec((1, d), const2),
            pl.BlockSpec((ROUTER_ROWS, d), const2),
            pl.BlockSpec((ROUTER_ROWS, 1), const2),
        ],
        out_specs=[
            pl.BlockSpec((1, TOP_K, tm), out3),
            pl.BlockSpec((1, TOP_K, tm), out3),
            pl.BlockSpec((1, 2 * N_EXPERTS, LANES), out3),
        ],
        out_shape=[
            jax.ShapeDtypeStruct((n_tiles, TOP_K, tm), jnp.int32),
            jax.ShapeDtypeStruct((n_tiles, TOP_K, tm), _F32),
            jax.ShapeDtypeStruct((n_tiles, 2 * N_EXPERTS, LANES), jnp.int32),
        ],
        scratch_shapes=[
            pltpu.VMEM((nb, TOP_K, tb), jnp.int32),
            pltpu.VMEM((nb, TOP_K, tb), _F32),
            pltpu.VMEM((nb, TOP_K, tb), _F32),
            pltpu.VMEM((N_EXPERTS, LANES), _F32),
        ],
        compiler_params=pltpu.CompilerParams(
            dimension_semantics=("arbitrary", "arbitrary"), vmem_limit_bytes=VMEM_LIMIT),
        name="route",
    )(xf, sh, sc, norm_g.reshape(1, d), w_rt, b_rt)


def _row_ds(offset):
    return pl.ds(pl.multiple_of(offset, ROW_VREGS), ROW_VREGS)


def _moe_kernel(pos_s, wgt_s, seg_s,
                x_ref, sh_ref, sc_ref, gt_ref, ng_ref, fg_ref, wg_hbm, wu_hbm, wd_hbm, o_ref,
                tm_ref, srt_ref, wgu_buf, wd_buf, sem, *, layer, final_norm):
    tm, tb, nb, ch = MOE_TILE, MOE_BLOCK, MOE_BLOCKS, MOE_CHUNK
    i = pl.program_id(0)
    j = pl.program_id(1)
    base = i * (TOP_K * tm) + (j % nb) * tb

    def weight_copies(e, slot):
        return (pltpu.make_async_copy(wg_hbm.at[layer, e], wgu_buf.at[slot, :, 0:D_EXPERT], sem.at[0, slot]),
                pltpu.make_async_copy(wu_hbm.at[layer, e], wgu_buf.at[slot, :, D_EXPERT:], sem.at[1, slot]),
                pltpu.make_async_copy(wd_hbm.at[layer, e], wd_buf.at[slot], sem.at[2, slot]))

    @pl.when(j == 0)
    def _():
        for e0 in range(W_SLOTS - 1):
            for cp in weight_copies(e0, e0):
                cp.start()
        srt_ref[TOP_K * tm * ROW_VREGS:, :] = jnp.zeros((MOE_PAD_ROWS * ROW_VREGS, LANES), _F32)

    @pl.when(j < nb)
    def _():
        h = _rms_norm(x_ref[...], ng_ref[...]) * (1.0 + sc_ref[0]) + sh_ref[0]
        for s in range(ROW_VREGS):
            tm_ref[pl.ds(s, tb, stride=ROW_VREGS), :] = h[:, s * LANES:(s + 1) * LANES]

        def dispatch(it, carry):
            t0 = it * UNROLL
            rows = [tm_ref[_row_ds((t0 + u) * ROW_VREGS), :] for u in range(UNROLL)]
            for u in range(UNROLL):
                for k in range(TOP_K):
                    srt_ref[_row_ds(pos_s[base + k * tm + t0 + u]), :] = rows[u]
            return carry

        lax.fori_loop(0, tb // UNROLL, dispatch, 0)

    @pl.when(j == nb)
    def _():
        def expert(e, carry):
            slot = e % W_SLOTS
            for cp in weight_copies(e, slot):
                cp.wait()

            @pl.when(e + W_SLOTS - 1 < N_EXPERTS)
            def _():
                for cp in weight_copies(e + W_SLOTS - 1, (e + W_SLOTS - 1) % W_SLOTS):
                    cp.start()

            off = seg_s[i * (2 * N_EXPERTS) + e]
            cnt = seg_s[i * (2 * N_EXPERTS) + N_EXPERTS + e]

            def run_rows(first_row, n_sub):
                subs = range(n_sub)
                row0 = [first_row + sub * MOE_SUB for sub in subs]
                r0 = [(off + r) * ROW_VREGS for r in row0]
                xs = [jnp.concatenate([srt_ref[pl.ds(r0[sub] + s, MOE_SUB, stride=ROW_VREGS), :]
                                       for s in range(ROW_VREGS)], axis=1) for sub in subs]
                ys = []
                hgus = [jnp.dot(xs[sub].astype(_BF16), wgu_buf[slot], preferred_element_type=_F32)
                        for sub in subs]
                for sub in subs:
                    hgu = hgus[sub]
                    act = _silu(hgu[:, :D_EXPERT]) * hgu[:, D_EXPERT:]
                    y = jnp.dot(act.astype(_BF16), wd_buf[slot], preferred_element_type=_F32)
                    valid = lax.broadcasted_iota(jnp.int32, (MOE_SUB, 1), 0) < (cnt - row0[sub])
                    ys.append(jnp.where(valid, y, xs[sub]))
                for sub in subs:
                    for s in range(ROW_VREGS):
                        srt_ref[pl.ds(r0[sub] + s, MOE_SUB, stride=ROW_VREGS), :] = (
                            ys[sub][:, s * LANES:(s + 1) * LANES])

            def chunk(c, inner):
                run_rows(c * ch, ch // MOE_SUB)
                return inner

            n_whole = jnp.maximum(cnt - MOE_TAIL_SUBS * MOE_SUB + ch - 1, 0) // ch
            lax.fori_loop(0, n_whole, chunk, 0)
            rest = cnt - n_whole * ch
            for n_sub in range(1, MOE_TAIL_SUBS + 1):
                @pl.when((rest > (n_sub - 1) * MOE_SUB) & (rest <= n_sub * MOE_SUB))
                def _(n_sub=n_sub):
                    run_rows(n_whole * ch, n_sub)
            return carry

        lax.fori_loop(0, N_EXPERTS, expert, 0)

    @pl.when(j >= nb)
    def _():
        def combine(it, carry):
            t0 = it * UNROLL
            outs = []
            for u in range(UNROLL):
                t = base + t0 + u
                y0 = srt_ref[_row_ds(pos_s[t]), :]
                y1 = srt_ref[_row_ds(pos_s[t + tm]), :]
                outs.append(wgt_s[t] * y0 + wgt_s[t + tm] * y1)
            for u in range(UNROLL):
                tm_ref[_row_ds((t0 + u) * ROW_VREGS), :] = outs[u]
            return carry

        lax.fori_loop(0, tb // UNROLL, combine, 0)

        moe = jnp.concatenate(
            [tm_ref[pl.ds(s, tb, stride=ROW_VREGS), :] for s in range(ROW_VREGS)], axis=1)
        y = x_ref[...] + gt_ref[0] * moe
        if final_norm:
            y = _rms_norm(y, fg_ref[...])
        o_ref[...] = y


def _moe(x, sh, sc, gt, layer, norm_g, w_rt, b_rt, w_g, w_u, w_d, final_g, final_norm):
    b, s, d = x.shape
    t = b * s
    tm, tb, nb = MOE_TILE, MOE_BLOCK, MOE_BLOCKS
    per_b = s // tm
    n_tiles = t // tm
    xf = x.reshape(t, d)
    pos, wgt, seg = _route(xf, sh, sc, norm_g, w_rt, b_rt, per_b)
    const2 = lambda i, j, *_: (0, 0)
    mod_spec = pl.BlockSpec((1, 1, d), lambda i, j, *_: (i // per_b, 0, 0))
    out = pl.pallas_call(
        functools.partial(_moe_kernel, layer=layer, final_norm=final_norm),
        grid_spec=pltpu.PrefetchScalarGridSpec(
            num_scalar_prefetch=3,
            grid=(n_tiles, 2 * nb),
            in_specs=[
                pl.BlockSpec((tb, d), lambda i, j, *_: (i * nb + j % nb, 0)),
                mod_spec, mod_spec, mod_spec,
                pl.BlockSpec((1, d), const2),
                pl.BlockSpec((1, d), const2),
                pl.BlockSpec(memory_space=pl.ANY),
                pl.BlockSpec(memory_space=pl.ANY),
                pl.BlockSpec(memory_space=pl.ANY),
            ],
            out_specs=pl.BlockSpec((tb, d), lambda i, j, *_: (i * nb + jnp.maximum(j - nb, 0), 0)),
            scratch_shapes=[
                pltpu.VMEM((tb * ROW_VREGS, LANES), _F32),
                pltpu.VMEM(((TOP_K * tm + MOE_PAD_ROWS) * ROW_VREGS, LANES), _F32),
                pltpu.VMEM((W_SLOTS, d, 2 * D_EXPERT), _BF16),
                pltpu.VMEM((W_SLOTS, D_EXPERT, d), _BF16),
                pltpu.SemaphoreType.DMA((3, W_SLOTS)),
            ],
        ),
        out_shape=jax.ShapeDtypeStruct((t, d), _F32),
        compiler_params=pltpu.CompilerParams(
            dimension_semantics=("arbitrary", "arbitrary"), vmem_limit_bytes=VMEM_LIMIT),
        name="moe",
    )(pos.reshape(-1), wgt.reshape(-1), seg[:, :, 0].reshape(-1),
      xf, sh, sc, gt, norm_g.reshape(1, d), final_g.reshape(1, d), w_g, w_u, w_d)
    return out.reshape(b, s, d)


def _router_params(w_rg, b_rg, w_re, b_re):
    d = w_rg.shape[0]
    pad = ROUTER_ROWS - N_EXPERTS - N_GROUPS
    w = jnp.concatenate([w_re.T, w_rg.T, jnp.zeros((pad, d), _F32)], axis=0)
    b = jnp.concatenate([b_re, b_rg, jnp.zeros((pad,), _F32)]).reshape(ROUTER_ROWS, 1)
    return w, b


def kernel(x, c, mixer_norm_g, w_ada, b_ada, w_in, conv_w, conv_b, conv_ln_g, conv_ln_b, w_conv_out,
           b_conv_out, pool_w, pool_scale, w_out, ffn_norm_g, w_router_group, b_router_group,
           w_router_expert, b_router_expert, w_expert_gate, w_expert_up, w_expert_down, final_norm_g):
    depth = w_ada.shape[0]
    bsz = x.shape[0]
    mod = _ada(c, w_ada, b_ada)
    mod = mod.reshape(depth, bsz, N_MOD, 1, D_MODEL)
    w_in_b, w_co_b, pool_w_b, w_out_b = (a.astype(_BF16) for a in (w_in, w_conv_out, pool_w, w_out))
    w_eg_b, w_eu_b, w_ed_b = (a.astype(_BF16) for a in (w_expert_gate, w_expert_up, w_expert_down))
    for l in range(depth):
        sh1, sc1, g1, sh2, sc2, g2 = (mod[l, :, i] for i in range(N_MOD))
        x = _mixer(x, sh1, sc1, g1, l, mixer_norm_g[l], w_in_b, conv_w[l], conv_b[l],
                   conv_ln_g[l], conv_ln_b[l], w_co_b, b_conv_out[l], pool_w_b, pool_scale[l], w_out_b)
        w_rt, b_rt = _router_params(w_router_group[l], b_router_group[l],
                                    w_router_expert[l], b_router_expert[l])
        x = _moe(x, sh2, sc2, g2, l, ffn_norm_g[l], w_rt, b_rt, w_eg_b, w_eu_b, w_ed_b,
                 final_norm_g, final_norm=(l == depth - 1))
    return x
```

```python
import functools

import jax
import jax.numpy as jnp
from jax import lax
from jax.experimental import pallas as pl
from jax.experimental.pallas import tpu as pltpu

D_MODEL = 1024
D_CONV = 512
D_POOL = 512
CONV_K = 31
POOL_WINDOWS = (2, 4, 8, 16)
POOL_GROUP = 128
POOL_OUT_GROUP = 256
N_GROUPS = 4
PER_GROUP = 8
N_EXPERTS = 32
TOP_K = 2
D_EXPERT = 256
N_MOD = 6
EPS = 1e-6

COL_POOL = 2 * D_CONV
COL_GATE_A = COL_POOL + D_POOL
COL_GATE_B = COL_GATE_A + D_MODEL
D_IN = COL_GATE_B + D_MODEL

LANES = 128
SUBLANES = 8
ROW_VREGS = D_MODEL // LANES
HALO = 32
CONV_HALO_VREGS = 32
SEQ_TILE = 512
CONV_ROWS = 128
CONV_ROW_STEPS = SEQ_TILE // CONV_ROWS
CONV_LANE_GROUPS = D_CONV // LANES
CONV_ACCS = 4
TAIL_ROWS = 256
GATE_COLS = 256
GATE_COL_BLOCKS = (D_IN - COL_GATE_A) // GATE_COLS
MOE_TILE = 4096
MOE_BLOCK = 512
MOE_BLOCKS = MOE_TILE // MOE_BLOCK
ROUTE_BLOCK = 2048
ROUTE_BLOCKS = MOE_TILE // ROUTE_BLOCK
MOE_CHUNK = 256
MOE_SUB = 128
MOE_TAIL_SUBS = 3
MOE_PAD_ROWS = MOE_SUB
W_SLOTS = 4
CUM_BLOCK = 256
ROUTER_ROWS = 48
GROUP_ROW0 = N_EXPERTS
UNROLL = 32
VMEM_LIMIT = 56 * 1024 * 1024
MIXER_OPERANDS = 15
MIXER_WEIGHT_OPERANDS = (5, 10, 12, 14)

_F32 = jnp.float32
_BF16 = jnp.bfloat16
assert ROW_VREGS == SUBLANES


def _sigmoid(v):
    return 1.0 / (1.0 + jnp.exp(-v))


def _silu(v):
    return v * _sigmoid(v)


def _rms_norm(v, g):
    return v * lax.rsqrt(jnp.mean(v * v, axis=-1, keepdims=True) + EPS) * g


def _ada_kernel(c_ref, w_ref, b_ref, o_ref):
    c_act = _silu(c_ref[...])
    o_ref[0] = jnp.dot(c_act, w_ref[0], preferred_element_type=_F32,
                       precision=lax.Precision.HIGHEST) + b_ref[0]


def _ada(c, w_ada, b_ada):
    depth, d, n = w_ada.shape
    tn = 1024
    return pl.pallas_call(
        _ada_kernel,
        grid=(depth, n // tn),
        in_specs=[
            pl.BlockSpec(c.shape, lambda l, j: (0, 0)),
            pl.BlockSpec((1, d, tn), lambda l, j: (l, 0, j)),
            pl.BlockSpec((1, 1, tn), lambda l, j: (l, 0, j)),
        ],
        out_specs=pl.BlockSpec((1, c.shape[0], tn), lambda l, j: (l, 0, j)),
        out_shape=jax.ShapeDtypeStruct((depth, c.shape[0], n), _F32),
        name="ada_mod",
    )(c, w_ada, b_ada.reshape(depth, 1, n))


def _mixer_kernel(x_ref, sh_ref, sc_ref, gt_ref, ng_ref, win_ref, cw_ref, cb_ref, lng_ref, lnb_ref,
                  wco_ref, bco_ref, pw_ref, ps_ref, wout_ref, o_ref,
                  gates_ref, xil_ref, prev_ref, conv_ref, pe_ref, p2_ref, p4_ref, p8_ref):
    ts = SEQ_TILE
    seg = ts // SUBLANES
    halo_rows = CONV_HALO_VREGS * SUBLANES
    j = pl.program_id(1)

    @pl.when(j == 0)
    def _():
        prev_ref[...] = jnp.zeros_like(prev_ref)
        pe_ref[0:HALO, :] = jnp.zeros((HALO, D_POOL), _F32)

    @pl.when(j > 0)
    def _():
        prev_ref[...] = xil_ref[:, seg * SUBLANES:seg * SUBLANES + halo_rows, :]
        pe_ref[0:HALO, :] = pe_ref[ts:ts + HALO, :]

    xt = x_ref[0]
    h = _rms_norm(xt, ng_ref[...]) * (1.0 + sc_ref[0]) + sh_ref[0]
    hb = h.astype(_BF16)

    a = jnp.dot(hb, win_ref[:, 0:D_CONV], preferred_element_type=_F32)
    g = jnp.dot(hb, win_ref[:, D_CONV:2 * D_CONV], preferred_element_type=_F32)
    glu = a * _sigmoid(g)
    sub = lax.broadcasted_iota(jnp.int32, (CONV_HALO_VREGS, SUBLANES, LANES), 1)
    for c in range(CONV_LANE_GROUPS):
        for s in range(SUBLANES):
            xil_ref[c, pl.ds(halo_rows + s, seg, stride=SUBLANES), :] = (
                glu[s * seg:(s + 1) * seg, c * LANES:(c + 1) * LANES])
        tail = xil_ref[c, seg * SUBLANES:seg * SUBLANES + halo_rows, :]
        tail = pltpu.roll(tail.reshape(CONV_HALO_VREGS, SUBLANES, LANES), 1, 1)
        prev = pltpu.roll(prev_ref[c].reshape(CONV_HALO_VREGS, SUBLANES, LANES), 1, 1)
        xil_ref[c, 0:halo_rows, :] = jnp.where(sub == 0, prev, tail).reshape(halo_rows, LANES)

    def conv_block(c, r0, zero_row):
        accs = [jnp.broadcast_to(cb_ref[c] + zero_row, (CONV_ROWS, LANES))] + [None] * (CONV_ACCS - 1)
        for k in range(CONV_K):
            lo = r0 + (CONV_HALO_VREGS - (CONV_K - 1) + k) * SUBLANES
            tap = xil_ref[c, lo:lo + CONV_ROWS, :]
            prod = tap * cw_ref[k, c]
            a_i = k % CONV_ACCS
            accs[a_i] = prod if accs[a_i] is None else accs[a_i] + prod
        while len(accs) > 1:
            accs = [accs[p] + accs[p + 1] for p in range(0, len(accs), 2)]
        conv_ref[c, r0:r0 + CONV_ROWS, :] = accs[0]

    n_conv = CONV_LANE_GROUPS * CONV_ROW_STEPS
    convs_per_gate = n_conv // GATE_COL_BLOCKS
    for it in range(n_conv):
        if it % convs_per_gate == 0:
            n = it // convs_per_gate
            col = COL_GATE_A + n * GATE_COLS
            gate = _sigmoid(jnp.dot(hb, win_ref[:, col:col + GATE_COLS], preferred_element_type=_F32))
            gates_ref[n] = gate
            zero_row = jnp.where(gate[ts - 1:ts, GATE_COLS - LANES:] > 2.0, 1.0, 0.0)
        conv_block(it // CONV_ROW_STEPS, (it % CONV_ROW_STEPS) * CONV_ROWS, zero_row)

    u = jnp.dot(hb, win_ref[:, COL_POOL:COL_GATE_A], preferred_element_type=_F32)
    pe_ref[HALO:HALO + ts, :] = u
    n_ext = ts + HALO
    p2_ref[8:n_ext, :] = pe_ref[8:n_ext, :] + pe_ref[7:n_ext - 1, :]
    p4_ref[16:n_ext, :] = p2_ref[16:n_ext, :] + p2_ref[14:n_ext - 2, :]
    p8_ref[24:n_ext, :] = p4_ref[24:n_ext, :] + p4_ref[20:n_ext - 4, :]

    half = GATE_COL_BLOCKS // 2
    for rb in range(ts // TAIL_ROWS):
        r0 = rb * TAIL_ROWS
        rows = slice(r0, r0 + TAIL_ROWS)
        ext = slice(HALO + r0, HALO + r0 + TAIL_ROWS)
        v = jnp.concatenate(
            [jnp.concatenate([conv_ref[c, pl.ds(s, seg, stride=SUBLANES), :]
                              for s in range(r0 // seg, (r0 + TAIL_ROWS) // seg)], axis=0)
             for c in range(CONV_LANE_GROUPS)], axis=1)
        mu = jnp.mean(v, axis=-1, keepdims=True)
        vc = v - mu
        var = jnp.mean(vc * vc, axis=-1, keepdims=True)
        v = _silu(vc * lax.rsqrt(var + EPS) * lng_ref[...] + lnb_ref[...])
        y_a = jnp.dot(v.astype(_BF16), wco_ref[...], preferred_element_type=_F32) + bco_ref[...]

        pos = j * ts + r0 + lax.broadcasted_iota(jnp.int32, (TAIL_ROWS, 1), 0) + 1
        y_b_parts = []
        for gi, w in enumerate(POOL_WINDOWS):
            cs = slice(gi * POOL_GROUP, (gi + 1) * POOL_GROUP)
            if w == 2:
                win_sum = p2_ref[ext, cs]
            elif w == 4:
                win_sum = p4_ref[ext, cs]
            elif w == 8:
                win_sum = p8_ref[ext, cs]
            else:
                win_sum = p8_ref[ext, cs] + p8_ref[HALO + r0 - 8:HALO + r0 - 8 + TAIL_ROWS, cs]
            count = jnp.minimum(pos, w).astype(_F32)
            pooled = win_sum / count - pe_ref[ext, cs]
            y_b_parts.append(jnp.dot(pooled.astype(_BF16), pw_ref[gi], preferred_element_type=_F32))
        y_b = jnp.concatenate(y_b_parts, axis=-1) * ps_ref[...]

        gate_a = jnp.concatenate([gates_ref[n, rows, :] for n in range(half)], axis=1)
        gate_b = jnp.concatenate([gates_ref[n, rows, :] for n in range(half, GATE_COL_BLOCKS)], axis=1)
        mixed = gate_a * y_a + gate_b * y_b
        out = jnp.dot(mixed.astype(_BF16), wout_ref[...], preferred_element_type=_F32)
        o_ref[0, rows, :] = x_ref[0, rows, :] + gt_ref[0] * out


def _mixer(x, sh, sc, gt, layer, norm_g, w_in, conv_w, conv_b, ln_g, ln_b, w_co, b_co, pool_w, pool_scale,
           w_out):
    b, s, d = x.shape
    ts = SEQ_TILE
    const2 = lambda bi, j: (0, 0)
    const3 = lambda bi, j: (0, 0, 0)
    row = lambda a: a.reshape(1, -1)

    def layer_spec(a):
        return pl.BlockSpec((None,) + a.shape[1:], lambda bi, j: (layer,) + (0,) * (a.ndim - 1))
    mod_spec = pl.BlockSpec((1, 1, d), lambda bi, j: (bi, 0, 0))
    conv_w = conv_w.reshape(CONV_K, CONV_LANE_GROUPS, 1, LANES)
    conv_b = conv_b.reshape(CONV_LANE_GROUPS, 1, LANES)
    return pl.pallas_call(
        _mixer_kernel,
        grid=(b, s // ts),
        in_specs=[
            pl.BlockSpec((1, ts, d), lambda bi, j: (bi, j, 0)),
            mod_spec, mod_spec, mod_spec,
            pl.BlockSpec((1, d), const2),
            layer_spec(w_in),
            pl.BlockSpec(conv_w.shape, lambda bi, j: (0, 0, 0, 0)),
            pl.BlockSpec(conv_b.shape, const3),
            pl.BlockSpec((1, D_CONV), const2),
            pl.BlockSpec((1, D_CONV), const2),
            layer_spec(w_co),
            pl.BlockSpec((1, d), const2),
            layer_spec(pool_w),
            pl.BlockSpec((1, d), const2),
            layer_spec(w_out),
        ],
        out_specs=pl.BlockSpec((1, ts, d), lambda bi, j: (bi, j, 0)),
        out_shape=jax.ShapeDtypeStruct(x.shape, _F32),
        scratch_shapes=[
            pltpu.VMEM((GATE_COL_BLOCKS, ts, GATE_COLS), _F32),
            pltpu.VMEM((CONV_LANE_GROUPS, ts + CONV_HALO_VREGS * SUBLANES, LANES), _F32),
            pltpu.VMEM((CONV_LANE_GROUPS, CONV_HALO_VREGS * SUBLANES, LANES), _F32),
            pltpu.VMEM((CONV_LANE_GROUPS, ts, LANES), _F32),
            pltpu.VMEM((ts + HALO, D_POOL), _F32),
            pltpu.VMEM((ts + HALO, D_POOL), _F32),
            pltpu.VMEM((ts + HALO, D_POOL), _F32),
            pltpu.VMEM((ts + HALO, D_POOL), _F32),
        ],
        compiler_params=pltpu.CompilerParams(
            dimension_semantics=("arbitrary", "arbitrary"), vmem_limit_bytes=VMEM_LIMIT,
            allow_input_fusion=[i in MIXER_WEIGHT_OPERANDS for i in range(MIXER_OPERANDS)]),
        name="mixer",
    )(x, sh, sc, gt, row(norm_g), w_in, conv_w, conv_b, row(ln_g), row(ln_b), w_co, row(b_co),
      pool_w, row(pool_scale), w_out)


def _route_kernel(x_ref, sh_ref, sc_ref, ng_ref, wrt_ref, brt_ref, pos_ref, wgt_ref, seg_ref,
                  id_scr, rank_scr, wgt_scr, run_ref):
    tb = ROUTE_BLOCK
    j = pl.program_id(1)

    @pl.when(j == 0)
    def _():
        run_ref[...] = jnp.zeros_like(run_ref)

    h = _rms_norm(x_ref[...], ng_ref[...]) * (1.0 + sc_ref[0]) + sh_ref[0]
    nt = (((1,), (1,)), ((), ()))
    h_hi = h.astype(_BF16)
    h_lo = (h - h_hi.astype(_F32)).astype(_BF16)
    w = wrt_ref[...]
    w_hi = w.astype(_BF16)
    w_lo = (w - w_hi.astype(_F32)).astype(_BF16)
    part = lax.dot_general(jnp.concatenate([w_hi, w_lo], axis=0), h_hi, nt, preferred_element_type=_F32)
    logits = (part[0:ROUTER_ROWS] + part[ROUTER_ROWS:]
              + lax.dot_general(w_hi, h_lo, nt, preferred_element_type=_F32)
              + brt_ref[...])
    neg = jnp.float32(-jnp.inf)
    big = jnp.int32(N_EXPERTS)

    grow = lax.broadcasted_iota(jnp.int32, (SUBLANES, tb), 0)
    glog = jnp.where(grow < N_GROUPS, logits[GROUP_ROW0:GROUP_ROW0 + SUBLANES, :], neg)
    gmax = jnp.max(glog, axis=0, keepdims=True)
    p_group = 1.0 / jnp.sum(jnp.exp(glog - gmax), axis=0, keepdims=True)
    g_idx = jnp.min(jnp.where(glog == gmax, grow, big), axis=0, keepdims=True)

    erow = lax.broadcasted_iota(jnp.int32, (N_EXPERTS, tb), 0)
    el = jnp.where((erow // PER_GROUP) == g_idx, logits[0:N_EXPERTS, :], neg)
    m1 = jnp.max(el, axis=0, keepdims=True)
    i1 = jnp.min(jnp.where(el == m1, erow, big), axis=0, keepdims=True)
    el2 = jnp.where(erow == i1, neg, el)
    m2 = jnp.max(el2, axis=0, keepdims=True)
    i2 = jnp.min(jnp.where(el2 == m2, erow, big), axis=0, keepdims=True)
    r = jnp.exp(m2 - m1)
    w1 = p_group / (1.0 + r)
    w2 = w1 * r
    sel1 = erow == i1
    sel2 = erow == i2

    used = jnp.where(sel1 | sel2, 1.0, 0.0)
    tri_r = lax.broadcasted_iota(jnp.int32, (CUM_BLOCK, CUM_BLOCK), 0)
    tri_c = lax.broadcasted_iota(jnp.int32, (CUM_BLOCK, CUM_BLOCK), 1)
    upper = jnp.where(tri_r < tri_c, 1.0, 0.0).astype(_BF16)
    run = run_ref[:, 0:1]
    ranks = []
    for blk in range(tb // CUM_BLOCK):
        ub = used[:, blk * CUM_BLOCK:(blk + 1) * CUM_BLOCK]
        ranks.append(jnp.dot(ub.astype(_BF16), upper, preferred_element_type=_F32) + run)
        run = run + jnp.sum(ub, axis=1, keepdims=True)
    rank = jnp.concatenate(ranks, axis=1)
    run_ref[...] = jnp.broadcast_to(run, run_ref.shape)
    id_scr[j, 0:1, :] = i1
    id_scr[j, 1:2, :] = i2
    rank_scr[j, 0:1, :] = jnp.sum(jnp.where(sel1, rank, 0.0), axis=0, keepdims=True)
    rank_scr[j, 1:2, :] = jnp.sum(jnp.where(sel2, rank, 0.0), axis=0, keepdims=True)
    wgt_scr[j, 0:1, :] = w1
    wgt_scr[j, 1:2, :] = w2

    @pl.when(j == ROUTE_BLOCKS - 1)
    def _():
        cnt = jnp.concatenate([run_ref[...], jnp.zeros((LANES - N_EXPERTS, LANES), _F32)], axis=0)
        low_r = lax.broadcasted_iota(jnp.int32, (LANES, LANES), 0)
        low_c = lax.broadcasted_iota(jnp.int32, (LANES, LANES), 1)
        lower = jnp.where(low_c < low_r, 1.0, 0.0)
        off = jnp.dot(lower, cnt, preferred_element_type=_F32, precision=lax.Precision.HIGHEST)
        seg_ref[0, 0:N_EXPERTS, :] = off[0:N_EXPERTS, :].astype(jnp.int32)
        seg_ref[0, N_EXPERTS:2 * N_EXPERTS, :] = cnt[0:N_EXPERTS, :].astype(jnp.int32)
        off_col = off[0:N_EXPERTS, 0:1]
        for blk in range(ROUTE_BLOCKS):
            ls = slice(blk * tb, (blk + 1) * tb)
            for k in range(TOP_K):
                start = jnp.sum(jnp.where(erow == id_scr[blk, k:k + 1, :], off_col, 0.0),
                                axis=0, keepdims=True)
                slot = start + rank_scr[blk, k:k + 1, :]
                pos_ref[0, k:k + 1, ls] = (slot * ROW_VREGS).astype(jnp.int32)
                wgt_ref[0, k:k + 1, ls] = wgt_scr[blk, k:k + 1, :]


def _route(xf, sh, sc, norm_g, w_rt, b_rt, per_b):
    t, d = xf.shape
    tm, tb, nb = MOE_TILE, ROUTE_BLOCK, ROUTE_BLOCKS
    n_tiles = t // tm
    const2 = lambda i, j: (0, 0)
    mod_spec = pl.BlockSpec((1, 1, d), lambda i, j: (i // per_b, 0, 0))
    out3 = lambda i, j: (i, 0, 0)
    return pl.pallas_call(
        _route_kernel,
        grid=(n_tiles, nb),
        in_specs=[
            pl.BlockSpec((tb, d), lambda i, j: (i * nb + j, 0)),
            mod_spec, mod_spec,
            pl.BlockSpec((1, d), const2),
            pl.BlockSpec((ROUTER_ROWS, d), const2),
            pl.BlockSpec((ROUTER_ROWS, 1), const2),
        ],
        out_specs=[
            pl.BlockSpec((1, TOP_K, tm), out3),
            pl.BlockSpec((1, TOP_K, tm), out3),
            pl.BlockSpec((1, 2 * N_EXPERTS, LANES), out3),
        ],
        out_shape=[
            jax.ShapeDtypeStruct((n_tiles, TOP_K, tm), jnp.int32),
            jax.ShapeDtypeStruct((n_tiles, TOP_K, tm), _F32),
            jax.ShapeDtypeStruct((n_tiles, 2 * N_EXPERTS, LANES), jnp.int32),
        ],
        scratch_shapes=[
            pltpu.VMEM((nb, TOP_K, tb), jnp.int32),
            pltpu.VMEM((nb, TOP_K, tb), _F32),
            pltpu.VMEM((nb, TOP_K, tb), _F32),
            pltpu.VMEM((N_EXPERTS, LANES), _F32),
        ],
        compiler_params=pltpu.CompilerParams(
            dimension_semantics=("arbitrary", "arbitrary"), vmem_limit_bytes=VMEM_LIMIT),
        name="route",
    )(xf, sh, sc, norm_g.reshape(1, d), w_rt, b_rt)


def _row_ds(offset):
    return pl.ds(pl.multiple_of(offset, ROW_VREGS), ROW_VREGS)


def _moe_kernel(pos_s, wgt_s, seg_s,
                x_ref, sh_ref, sc_ref, gt_ref, ng_ref, fg_ref, wg_hbm, wu_hbm, wd_hbm, o_ref,
                tm_ref, srt_ref, wgu_buf, wd_buf, sem, *, layer, final_norm):
    tm, tb, nb, ch = MOE_TILE, MOE_BLOCK, MOE_BLOCKS, MOE_CHUNK
    i = pl.program_id(0)
    j = pl.program_id(1)
    base = i * (TOP_K * tm) + (j % nb) * tb

    def weight_copies(e, slot):
        return (pltpu.make_async_copy(wg_hbm.at[layer, e], wgu_buf.at[slot, :, 0:D_EXPERT], sem.at[0, slot]),
                pltpu.make_async_copy(wu_hbm.at[layer, e], wgu_buf.at[slot, :, D_EXPERT:], sem.at[1, slot]),
                pltpu.make_async_copy(wd_hbm.at[layer, e], wd_buf.at[slot], sem.at[2, slot]))

    @pl.when(j == 0)
    def _():
        for e0 in range(W_SLOTS - 1):
            for cp in weight_copies(e0, e0):
                cp.start()
        srt_ref[TOP_K * tm * ROW_VREGS:, :] = jnp.zeros((MOE_PAD_ROWS * ROW_VREGS, LANES), _F32)

    @pl.when(j < nb)
    def _():
        h = _rms_norm(x_ref[...], ng_ref[...]) * (1.0 + sc_ref[0]) + sh_ref[0]
        for s in range(ROW_VREGS):
            tm_ref[pl.ds(s, tb, stride=ROW_VREGS), :] = h[:, s * LANES:(s + 1) * LANES]

        def dispatch(it, carry):
            t0 = it * UNROLL
            rows = [tm_ref[_row_ds((t0 + u) * ROW_VREGS), :] for u in range(UNROLL)]
            for u in range(UNROLL):
                for k in range(TOP_K):
                    srt_ref[_row_ds(pos_s[base + k * tm + t0 + u]), :] = rows[u]
            return carry

        lax.fori_loop(0, tb // UNROLL, dispatch, 0)

    @pl.when(j == nb)
    def _():
        def expert(e, carry):
            slot = e % W_SLOTS
            for cp in weight_copies(e, slot):
                cp.wait()

            @pl.when(e + W_SLOTS - 1 < N_EXPERTS)
            def _():
                for cp in weight_copies(e + W_SLOTS - 1, (e + W_SLOTS - 1) % W_SLOTS):
                    cp.start()

            off = seg_s[i * (2 * N_EXPERTS) + e]
            cnt = seg_s[i * (2 * N_EXPERTS) + N_EXPERTS + e]

            def run_rows(first_row, n_sub):
                subs = range(n_sub)
                row0 = [first_row + sub * MOE_SUB for sub in subs]
                r0 = [(off + r) * ROW_VREGS for r in row0]
                xs = [jnp.concatenate([srt_ref[pl.ds(r0[sub] + s, MOE_SUB, stride=ROW_VREGS), :]
                                       for s in range(ROW_VREGS)], axis=1) for sub in subs]
                ys = []
                hgus = [jnp.dot(xs[sub].astype(_BF16), wgu_buf[slot], preferred_element_type=_F32)
                        for sub in subs]
                for sub in subs:
                    hgu = hgus[sub]
                    act = _silu(hgu[:, :D_EXPERT]) * hgu[:, D_EXPERT:]
                    y = jnp.dot(act.astype(_BF16), wd_buf[slot], preferred_element_type=_F32)
                    valid = lax.broadcasted_iota(jnp.int32, (MOE_SUB, 1), 0) < (cnt - row0[sub])
                    ys.append(jnp.where(valid, y, xs[sub]))
                for sub in subs:
                    for s in range(ROW_VREGS):
                        srt_ref[pl.ds(r0[sub] + s, MOE_SUB, stride=ROW_VREGS), :] = (
                            ys[sub][:, s * LANES:(s + 1) * LANES])

            def chunk(c, inner):
                run_rows(c * ch, ch // MOE_SUB)
                return inner

            n_whole = jnp.maximum(cnt - MOE_TAIL_SUBS * MOE_SUB + ch - 1, 0) // ch
            lax.fori_loop(0, n_whole, chunk, 0)
            rest = cnt - n_whole * ch
            for n_sub in range(1, MOE_TAIL_SUBS + 1):
                @pl.when((rest > (n_sub - 1) * MOE_SUB) & (rest <= n_sub * MOE_SUB))
                def _(n_sub=n_sub):
                    run_rows(n_whole * ch, n_sub)
            return carry

        lax.fori_loop(0, N_EXPERTS, expert, 0)

    @pl.when(j >= nb)
    def _():
        def combine(it, carry):
            t0 = it * UNROLL
            outs = []
            for u in range(UNROLL):
                t = base + t0 + u
                y0 = srt_ref[_row_ds(pos_s[t]), :]
                y1 = srt_ref[_row_ds(pos_s[t + tm]), :]
                outs.append(wgt_s[t] * y0 + wgt_s[t + tm] * y1)
            for u in range(UNROLL):
                tm_ref[_row_ds((t0 + u) * ROW_VREGS), :] = outs[u]
            return carry

        lax.fori_loop(0, tb // UNROLL, combine, 0)

        moe = jnp.concatenate(
            [tm_ref[pl.ds(s, tb, stride=ROW_VREGS), :] for s in range(ROW_VREGS)], axis=1)
        y = x_ref[...] + gt_ref[0] * moe
        if final_norm:
            y = _rms_norm(y, fg_ref[...])
        o_ref[...] = y


def _moe(x, sh, sc, gt, layer, norm_g, w_rt, b_rt, w_g, w_u, w_d, final_g, final_norm):
    b, s, d = x.shape
    t = b * s
    tm, tb, nb = MOE_TILE, MOE_BLOCK, MOE_BLOCKS
    per_b = s // tm
    n_tiles = t // tm
    xf = x.reshape(t, d)
    pos, wgt, seg = _route(xf, sh, sc, norm_g, w_rt, b_rt, per_b)
    const2 = lambda i, j, *_: (0, 0)
    mod_spec = pl.BlockSpec((1, 1, d), lambda i, j, *_: (i // per_b, 0, 0))
    out = pl.pallas_call(
        functools.partial(_moe_kernel, layer=layer, final_norm=final_norm),
        grid_spec=pltpu.PrefetchScalarGridSpec(
            num_scalar_prefetch=3,
            grid=(n_tiles, 2 * nb),
            in_specs=[
                pl.BlockSpec((tb, d), lambda i, j, *_: (i * nb + j % nb, 0)),
                mod_spec, mod_spec, mod_spec,
                pl.BlockSpec((1, d), const2),
                pl.BlockSpec((1, d), const2),
                pl.BlockSpec(memory_space=pl.ANY),
                pl.BlockSpec(memory_space=pl.ANY),
                pl.BlockSpec(memory_space=pl.ANY),
            ],
            out_specs=pl.BlockSpec((tb, d), lambda i, j, *_: (i * nb + jnp.maximum(j - nb, 0), 0)),
            scratch_shapes=[
                pltpu.VMEM((tb * ROW_VREGS, LANES), _F32),
                pltpu.VMEM(((TOP_K * tm + MOE_PAD_ROWS) * ROW_VREGS, LANES), _F32),
                pltpu.VMEM((W_SLOTS, d, 2 * D_EXPERT), _BF16),
                pltpu.VMEM((W_SLOTS, D_EXPERT, d), _BF16),
                pltpu.SemaphoreType.DMA((3, W_SLOTS)),
            ],
        ),
        out_shape=jax.ShapeDtypeStruct((t, d), _F32),
        compiler_params=pltpu.CompilerParams(
            dimension_semantics=("arbitrary", "arbitrary"), vmem_limit_bytes=VMEM_LIMIT),
        name="moe",
    )(pos.reshape(-1), wgt.reshape(-1), seg[:, :, 0].reshape(-1),
      xf, sh, sc, gt, norm_g.reshape(1, d), final_g.reshape(1, d), w_g, w_u, w_d)
    return out.reshape(b, s, d)


def _router_params(w_rg, b_rg, w_re, b_re):
    d = w_rg.shape[0]
    pad = ROUTER_ROWS - N_EXPERTS - N_GROUPS
    w = jnp.concatenate([w_re.T, w_rg.T, jnp.zeros((pad, d), _F32)], axis=0)
    b = jnp.concatenate([b_re, b_rg, jnp.zeros((pad,), _F32)]).reshape(ROUTER_ROWS, 1)
    return w, b


def kernel(x, c, mixer_norm_g, w_ada, b_ada, w_in, conv_w, conv_b, conv_ln_g, conv_ln_b, w_conv_out,
           b_conv_out, pool_w, pool_scale, w_out, ffn_norm_g, w_router_group, b_router_group,
           w_router_expert, b_router_expert, w_expert_gate, w_expert_up, w_expert_down, final_norm_g):
    depth = w_ada.shape[0]
    bsz = x.shape[0]
    mod = _ada(c, w_ada, b_ada)
    mod = mod.reshape(depth, bsz, N_MOD, 1, D_MODEL)
    w_in_b, w_co_b, pool_w_b, w_out_b = (a.astype(_BF16) for a in (w_in, w_conv_out, pool_w, w_out))
    w_eg_b, w_eu_b, w_ed_b = (a.astype(_BF16) for a in (w_expert_gate, w_expert_up, w_expert_down))
    for l in range(depth):
        sh1, sc1, g1, sh2, sc2, g2 = (mod[l, :, i] for i in range(N_MOD))
        x = _mixer(x, sh1, sc1, g1, l, mixer_norm_g[l], w_in_b, conv_w[l], conv_b[l],
                   conv_ln_g[l], conv_ln_b[l], w_co_b, b_conv_out[l], pool_w_b, pool_scale[l], w_out_b)
        w_rt, b_rt = _router_params(w_router_group[l], b_router_group[l],
                                    w_router_expert[l], b_router_expert[l])
        x = _moe(x, sh2, sc2, g2, l, ffn_norm_g[l], w_rt, b_rt, w_eg_b, w_eu_b, w_ed_b,
                 final_norm_g, final_norm=(l == depth - 1))
    return x
```
